```python
import math
import jax
import jax.numpy as jnp
from jax import lax
import numpy as np

D_MODEL = 2048
BATCH = 1
SEQ = 8192
DEPTH = 1

D_MIX = D_MODEL
M_HEADS = 4
M_QK_DIM = 128
M_V_DIM = D_MIX // 2 // M_HEADS
G_HEADS = 8
G_HEAD_DIM = D_MIX // 2 // G_HEADS
CONV_WIDTH = 5
CHUNK = 64
GATE_SOFTCAP = 15.0
N_EXPERTS = 64
TOP_K = 6
N_GROUPS = 8
TOPK_GROUPS = 4
D_EXPERT = 1408
D_SHARED = 1408
ROUTED_SCALE = 2.5
EXPERT_BLOCK = 256
NORM_EPS = 1e-6
DEEPNORM_ALPHA = (2.0 * DEPTH) ** 0.25
DEEPNORM_BETA = (8.0 * DEPTH) ** -0.25

M_QK = M_HEADS * M_QK_DIM
M_V = M_HEADS * M_V_DIM
G_W = G_HEADS * G_HEAD_DIM
IN_SPLIT = (M_QK, M_QK, M_V, M_V, 2 * M_HEADS, 2 * M_HEADS, 3 * G_W, G_W, 2 * G_HEADS, 2 * G_HEADS)
IN_COLS = sum(IN_SPLIT)

kernel_name = 'hybrid_mlstm_gdn_moe_deepnorm_adaln'


def layer_norm(x):
    x32 = x.astype(jnp.float32)
    mu = jnp.mean(x32, -1, keepdims=True)
    var = jnp.mean(jnp.square(x32 - mu), -1, keepdims=True)
    return ((x32 - mu) * lax.rsqrt(var + NORM_EPS)).astype(x.dtype)


def rms_norm(x):
    return x * lax.rsqrt(jnp.mean(jnp.square(x), -1, keepdims=True) + NORM_EPS)


def l2_norm(x):
    return x * lax.rsqrt(jnp.sum(jnp.square(x), -1, keepdims=True) + NORM_EPS)


def softcap(x):
    return GATE_SOFTCAP * jnp.tanh(x / GATE_SOFTCAP)


def split_cols(a, sizes):
    return jnp.split(a, np.cumsum(sizes)[:-1].tolist(), axis=-1)


def to_chunks(a):
    b, h, t = a.shape[:3]
    return jnp.moveaxis(a.reshape(b, h, t // CHUNK, CHUNK, *a.shape[3:]), 2, 0)


def from_chunks(a):
    nc, b, h, l = a.shape[:4]
    return jnp.moveaxis(a, 0, 2).reshape(b, h, nc * l, *a.shape[4:])


def centred_depthwise_conv(x, w):
    return lax.conv_general_dilated(
        x, w[:, None, :].astype(x.dtype), (1,), [(CONV_WIDTH // 2, CONV_WIDTH // 2)],
        dimension_numbers=('NWC', 'WIO', 'NWC'), feature_group_count=x.shape[-1])


def mlstm_chunkwise(q, k, v, log_i, log_f):
    b, h, _, dk = q.shape
    dv = v.shape[-1]
    tri = jnp.tril(jnp.ones((CHUNK, CHUNK), bool))

    def step(carry, inp):
        c_st, n_st, m_st = carry
        qc, kc, vc, ic, fc = inp
        bcum = jnp.cumsum(fc, axis=-1)
        d = jnp.where(tri, bcum[..., :, None] - bcum[..., None, :] + ic[..., None, :], -jnp.inf)
        inter = bcum + m_st[..., None]
        m_row = jnp.maximum(jnp.max(d, -1), inter)
        w = jnp.exp(d - m_row[..., None]) * jnp.einsum('bhid,bhjd->bhij', qc, kc)
        s_inter = jnp.exp(inter - m_row)
        num = jnp.einsum('bhij,bhje->bhie', w, vc) + s_inter[..., None] * jnp.einsum('bhid,bhde->bhie', qc, c_st)
        den = jnp.sum(w, -1) + s_inter * jnp.einsum('bhid,bhd->bhi', qc, n_st)
        out = num / jnp.maximum(jnp.abs(den), jnp.exp(-m_row))[..., None]
        b_last = bcum[..., -1]
        w_log = b_last[..., None] - bcum + ic
        m_new = jnp.maximum(b_last + m_st, jnp.max(w_log, -1))
        carry_decay = jnp.exp(b_last + m_st - m_new)
        w_add = jnp.exp(w_log - m_new[..., None])
        c_st = carry_decay[..., None, None] * c_st + jnp.einsum('bhj,bhjd,bhje->bhde', w_add, kc, vc)
        n_st = carry_decay[..., None] * n_st + jnp.einsum('bhj,bhjd->bhd', w_add, kc)
        return (c_st, n_st, m_new), out

    init = (jnp.zeros((b, h, dk, dv), jnp.float32), jnp.zeros((b, h, dk), jnp.float32),
            jnp.zeros((b, h), jnp.float32))
    _, out = lax.scan(step, init, (to_chunks(q), to_chunks(k), to_chunks(v), to_chunks(log_i), to_chunks(log_f)))
    return from_chunks(out)


def gated_delta_chunkwise(q, k, v, g, beta):
    qc, kc, vc, gc, bc = (to_chunks(a) for a in (q, k, v, g, beta))
    nc, b, h, l, dk = qc.shape
    dv = vc.shape[-1]
    tri = jnp.tril(jnp.ones((CHUNK, CHUNK), bool))
    strict = jnp.tril(jnp.ones((CHUNK, CHUNK), bool), -1)
    gcum = jnp.cumsum(gc, -1)
    decay = jnp.exp(jnp.where(tri, gcum[..., :, None] - gcum[..., None, :], -jnp.inf))
    kb = kc * bc[..., None]
    a_strict = jnp.where(strict, jnp.einsum('cbhid,cbhjd->cbhij', kb, kc) * decay, 0.0)
    rhs = jnp.concatenate([vc * bc[..., None], kb * jnp.exp(gcum)[..., None]], -1)
    sol = lax.linalg.triangular_solve(a_strict + jnp.eye(CHUNK, dtype=jnp.float32), rhs,
                                      left_side=True, lower=True, unit_diagonal=True)
    u, w = sol[..., :dv], sol[..., dv:]
    attn = jnp.einsum('cbhid,cbhjd->cbhij', qc, kc) * decay

    def step(s, inp):
        qj, kj, uj, wj, gj, aj = inp
        v_new = uj - jnp.einsum('bhid,bhde->bhie', wj, s)
        out = (jnp.einsum('bhid,bhde->bhie', qj * jnp.exp(gj)[..., None], s)
               + jnp.einsum('bhij,bhje->bhie', aj, v_new))
        g_last = gj[..., -1:]
        s = (s * jnp.exp(g_last)[..., None]
             + jnp.einsum('bhjd,bhje->bhde', kj * jnp.exp(g_last - gj)[..., None], v_new))
        return s, out

    _, out = lax.scan(step, jnp.zeros((b, h, dk, dv), jnp.float32), (qc, kc, u, w, gcum, attn))
    return from_chunks(out)


def bidirectional(scan_fn, q, k, v, gates):
    flip = lambda a: jnp.flip(a, axis=2)
    fwd = scan_fn(q, k, v, *(g[0] for g in gates))
    bwd = scan_fn(flip(q), flip(k), flip(v), *(flip(g[1]) for g in gates))
    return fwd + flip(bwd)


def hybrid_token_mixer(u, w_in, m_igate_bias, m_fgate_bias, m_norm_w, g_conv_w, g_A_log, g_dt_bias, g_norm_w, w_out):
    bsz, seq, _ = u.shape
    f32 = jnp.float32
    proj = u @ w_in
    mq, mk, mv, mo, mi, mf, gqkv, gz, gb, ga = split_cols(proj, IN_SPLIT)
    heads = lambda a, n, d: a.reshape(bsz, seq, n, d).transpose(0, 2, 1, 3).astype(f32)
    gates = lambda a, n: a.reshape(bsz, seq, 2, n).transpose(2, 0, 3, 1).astype(f32)
    q = heads(mq, M_HEADS, M_QK_DIM)
    k = heads(mk, M_HEADS, M_QK_DIM) * M_QK_DIM ** -0.5
    v = heads(mv, M_HEADS, M_V_DIM)
    log_i = softcap(gates(mi, M_HEADS) + m_igate_bias[:, None, :, None])
    log_f = jax.nn.log_sigmoid(softcap(gates(mf, M_HEADS) + m_fgate_bias[:, None, :, None]))
    hm = bidirectional(mlstm_chunkwise, q, k, v, (log_i, log_f))
    hm = rms_norm(hm).transpose(0, 2, 1, 3).reshape(bsz, seq, M_V) * m_norm_w * jax.nn.sigmoid(mo.astype(f32))
    qkv = jax.nn.silu(centred_depthwise_conv(gqkv, g_conv_w).astype(f32))
    gq, gk, gv = jnp.split(qkv, 3, axis=-1)
    q = l2_norm(heads(gq, G_HEADS, G_HEAD_DIM)) * G_HEAD_DIM ** -0.5
    k = l2_norm(heads(gk, G_HEADS, G_HEAD_DIM))
    v = heads(gv, G_HEADS, G_HEAD_DIM)
    beta = jax.nn.sigmoid(gates(gb, G_HEADS))
    log_decay = -jnp.exp(g_A_log.astype(f32))[:, None, :, None] * jax.nn.softplus(
        gates(ga, G_HEADS) + g_dt_bias[:, None, :, None])
    hg = bidirectional(gated_delta_chunkwise, q, k, v, (log_decay, beta))
    hg = (rms_norm(hg) * g_norm_w).transpose(0, 2, 1, 3).reshape(bsz, seq, G_W) * jax.nn.silu(gz.astype(f32))
    mixed = jnp.concatenate([hm, hg], -1).astype(u.dtype)
    return mixed @ w_out


def routed_experts(tok, top_idx, top_w, e_gate, e_up, e_down):
    n, d = tok.shape
    m = n * TOP_K
    n_blocks = (m + N_EXPERTS * (EXPERT_BLOCK - 1) + EXPERT_BLOCK - 1) // EXPERT_BLOCK
    flat_e = top_idx.reshape(-1)
    order = jnp.argsort(flat_e)
    e_sorted = flat_e[order]
    counts = jnp.bincount(flat_e, length=N_EXPERTS)
    padded = (counts + EXPERT_BLOCK - 1) // EXPERT_BLOCK * EXPERT_BLOCK
    start = jnp.cumsum(counts) - counts
    padded_end = jnp.cumsum(padded)
    dest = (padded_end - padded)[e_sorted] + jnp.arange(m) - start[e_sorted]
    n_slots = n_blocks * EXPERT_BLOCK
    slot_tok = jnp.zeros((n_slots,), jnp.int32).at[dest].set((order // TOP_K).astype(jnp.int32))
    slot_w = jnp.zeros((n_slots,), tok.dtype).at[dest].set(top_w.reshape(-1)[order].astype(tok.dtype))
    block_e = jnp.minimum(jnp.searchsorted(padded_end, jnp.arange(n_blocks) * EXPERT_BLOCK, side='right'),
                          N_EXPERTS - 1)

    def expert_block(args):
        idx, wts, e = args
        xb = tok[idx]
        hb = jax.nn.silu(xb @ e_gate[e]) * (xb @ e_up[e])
        return (hb @ e_down[e]) * wts[:, None]

    yb = lax.map(expert_block, (slot_tok.reshape(n_blocks, EXPERT_BLOCK),
                                slot_w.reshape(n_blocks, EXPERT_BLOCK), block_e))
    return jnp.zeros_like(tok).at[slot_tok].add(yb.reshape(-1, d))


def moe_ffn(u, router_w, router_bias, e_gate, e_up, e_down, s_gate, s_up, s_down):
    bsz, seq, d = u.shape
    f32 = jnp.float32
    tok = u.reshape(-1, d)
    n = tok.shape[0]
    scores = jax.nn.sigmoid((tok @ router_w).astype(f32))
    biased = scores + router_bias.astype(f32)
    group_score = jnp.sum(lax.top_k(biased.reshape(n, N_GROUPS, N_EXPERTS // N_GROUPS), 2)[0], -1)
    _, top_groups = lax.top_k(group_score, TOPK_GROUPS)
    group_mask = jnp.sum(jax.nn.one_hot(top_groups, N_GROUPS, dtype=f32), 1) > 0
    expert_mask = jnp.repeat(group_mask, N_EXPERTS // N_GROUPS, axis=1)
    _, top_idx = lax.top_k(jnp.where(expert_mask, biased, -jnp.inf), TOP_K)
    top_s = jnp.take_along_axis(scores, top_idx, axis=1)
    top_w = ROUTED_SCALE * top_s / jnp.sum(top_s, -1, keepdims=True)
    routed = routed_experts(tok, top_idx, top_w, e_gate, e_up, e_down)
    shared = (jax.nn.silu(tok @ s_gate) * (tok @ s_up)) @ s_down
    return (routed + shared).reshape(bsz, seq, d)


def setup_inputs(seed: int = 0) -> dict:
    key = jax.random.key(seed)
    ks = jax.random.split(key, 26)
    f32 = jnp.float32
    nrm = lambda k, shape, s: jax.random.normal(k, shape, f32) * s
    L, D, E = DEPTH, D_MODEL, N_EXPERTS
    beta = DEEPNORM_BETA
    col_scale = jnp.concatenate([
        jnp.ones((2 * M_QK,), f32), jnp.full((M_V,), beta, f32),
        jnp.ones((M_V + 4 * M_HEADS + 2 * G_W,), f32), jnp.full((G_W,), beta, f32),
        jnp.ones((G_W + 4 * G_HEADS,), f32)])
    dt = jnp.exp(jax.random.uniform(ks[10], (L, 2, G_HEADS), f32, math.log(1e-3), math.log(1e-1)))
    return {
        'x': nrm(ks[0], (BATCH, SEQ, D), 1.0),
        'c': nrm(ks[1], (BATCH, D), 1.0),
        'w_ada': nrm(ks[2], (L, D, 6 * D), 0.1 * D ** -0.5),
        'b_ada': nrm(ks[3], (L, 6 * D), 0.01),
        'w_in': nrm(ks[4], (L, D, IN_COLS), D ** -0.5) * col_scale,
        'm_igate_bias': nrm(ks[5], (L, 2, M_HEADS), 0.1),
        'm_fgate_bias': jnp.linspace(3.0, 6.0, M_HEADS, dtype=f32)[None, None, :] + nrm(ks[6], (L, 2, M_HEADS), 0.1),
        'm_norm_w': 1.0 + nrm(ks[7], (L, M_V), 0.02),
        'g_conv_w': nrm(ks[8], (L, CONV_WIDTH, 3 * G_W), CONV_WIDTH ** -0.5),
        'g_A_log': jnp.log(jax.random.uniform(ks[9], (L, 2, G_HEADS), f32, 1.0, 16.0)),
        'g_dt_bias': dt + jnp.log(-jnp.expm1(-dt)),
        'g_norm_w': 1.0 + nrm(ks[11], (L, G_HEAD_DIM), 0.02),
        'w_out': nrm(ks[12], (L, D_MIX, D), beta * D_MIX ** -0.5),
        'ln1_w': 1.0 + nrm(ks[13], (L, D), 0.02),
        'ln1_b': nrm(ks[14], (L, D), 0.02),
        'router_w': nrm(ks[15], (L, D, E), D ** -0.5),
        'router_bias': nrm(ks[16], (L, E), 0.01),
        'e_gate': nrm(ks[17], (L, E, D, D_EXPERT), D ** -0.5),
        'e_up': nrm(ks[18], (L, E, D, D_EXPERT), beta * D ** -0.5),
        'e_down': nrm(ks[19], (L, E, D_EXPERT, D), beta * D_EXPERT ** -0.5),
        's_gate': nrm(ks[20], (L, D, D_SHARED), D ** -0.5),
        's_up': nrm(ks[21], (L, D, D_SHARED), beta * D ** -0.5),
        's_down': nrm(ks[22], (L, D_SHARED, D), beta * D_SHARED ** -0.5),
        'ln2_w': 1.0 + nrm(ks[23], (L, D), 0.02),
        'ln2_b': nrm(ks[24], (L, D), 0.02),
    }


def reference(x, c, w_ada, b_ada, w_in, m_igate_bias, m_fgate_bias, m_norm_w, g_conv_w, g_A_log, g_dt_bias,
              g_norm_w, w_out, ln1_w, ln1_b, router_w, router_bias, e_gate, e_up, e_down, s_gate, s_up, s_down,
              ln2_w, ln2_b):
    h = x
    cond = jax.nn.silu(c)
    for layer in range(DEPTH):
        mod = (cond @ w_ada[layer] + b_ada[layer])[:, None, :]
        shift1, scale1, gate1, shift2, scale2, gate2 = jnp.split(mod, 6, axis=-1)
        u = layer_norm(h) * (1 + scale1) + shift1
        y = hybrid_token_mixer(u, w_in[layer], m_igate_bias[layer], m_fgate_bias[layer], m_norm_w[layer],
                               g_conv_w[layer], g_A_log[layer], g_dt_bias[layer], g_norm_w[layer], w_out[layer])
        h = layer_norm(DEEPNORM_ALPHA * h + (1 + gate1) * y) * ln1_w[layer] + ln1_b[layer]
        u = layer_norm(h) * (1 + scale2) + shift2
        y = moe_ffn(u, router_w[layer], router_bias[layer], e_gate[layer], e_up[layer], e_down[layer],
                    s_gate[layer], s_up[layer], s_down[layer])
        h = layer_norm(DEEPNORM_ALPHA * h + (1 + gate2) * y) * ln2_w[layer] + ln2_b[layer]
    return h
```

```python
import functools

import jax
import jax.numpy as jnp
from jax import lax
from jax.experimental import pallas as pl
from jax.experimental.pallas import tpu as pltpu

F32 = jnp.float32
BF16 = jnp.bfloat16

D_MODEL = 2048
M_HEADS = 4
M_QK_DIM = 128
M_V_DIM = 256
G_HEADS = 8
G_HEAD_DIM = 128
CONV_WIDTH = 5
CHUNK = 64
GATE_SOFTCAP = 15.0
N_EXPERTS = 64
TOP_K = 6
N_GROUPS = 8
TOPK_GROUPS = 4
D_EXPERT = 1408
ROUTED_SCALE = 2.5
EXPERT_BLOCK = 256
NORM_EPS = 1e-6
DEEPNORM_ALPHA = 2.0 ** 0.25

M_QK = M_HEADS * M_QK_DIM
M_V = M_HEADS * M_V_DIM
G_W = G_HEADS * G_HEAD_DIM
N_MAIN = 2 * M_QK + 2 * M_V + 3 * G_W + G_W
N_GATE = 2 * M_HEADS + 2 * M_HEADS + 2 * G_HEADS + 2 * G_HEADS
GATE_PAD = 128
COL_I = 0
COL_F = 8
COL_B = 16
COL_A = 32
OFF_MQ, OFF_MK, OFF_MV, OFF_MO = 0, 512, 1024, 2048
OFF_GQKV, OFF_GZ = 3072, 6144

VMEM_LIMIT = 56 * 1024 * 1024


def _cparams(sem):
    return pltpu.CompilerParams(dimension_semantics=sem, vmem_limit_bytes=VMEM_LIMIT)


def _dot(a, b):
    return jnp.dot(a, b, preferred_element_type=F32)


def _dot_nt(a, b):
    return lax.dot_general(a, b, (((1,), (1,)), ((), ())), preferred_element_type=F32)


def _dot_tn(a, b):
    return lax.dot_general(a, b, (((0,), (0,)), ((), ())), preferred_element_type=F32)


def _layer_norm(x):
    mu = jnp.mean(x, axis=-1, keepdims=True)
    xc = x - mu
    var = jnp.mean(xc * xc, axis=-1, keepdims=True)
    return xc * lax.rsqrt(var + NORM_EPS)


def _sigmoid(x):
    return 1.0 / (1.0 + jnp.exp(-x))


def _silu(x):
    return x * _sigmoid(x)


def _softplus(x):
    return jnp.maximum(x, 0.0) + jnp.log1p(jnp.exp(-jnp.abs(x)))


ADA_TN = 1024
ADA_RC = 64


def _ada_kernel(c_ref, w_ref, b_ref, o_ref):
    def body(r, acc):
        rows = pl.ds(pl.multiple_of(r * ADA_RC, ADA_RC), ADA_RC)
        cond = _silu(c_ref[rows, :])
        blk = w_ref[rows, :] * cond
        return acc + jnp.sum(blk.reshape(ADA_RC // 8, 8, ADA_TN), axis=0)

    acc = lax.fori_loop(0, D_MODEL // ADA_RC, body, jnp.zeros((8, ADA_TN), F32))
    o_ref[...] = jnp.sum(acc, axis=0, keepdims=True) + b_ref[...]


def _ada_mod(c, w_ada, b_ada):
    n = w_ada.shape[1]
    return pl.pallas_call(
        _ada_kernel,
        grid=(n // ADA_TN,),
        in_specs=[
            pl.BlockSpec((D_MODEL, 1), lambda j: (0, 0)),
            pl.BlockSpec((D_MODEL, ADA_TN), lambda j: (0, j)),
            pl.BlockSpec((1, ADA_TN), lambda j: (0, j)),
        ],
        out_specs=pl.BlockSpec((1, ADA_TN), lambda j: (0, j)),
        out_shape=jax.ShapeDtypeStruct((1, n), F32),
        compiler_params=_cparams(("arbitrary",)),
        name="ada_mod",
    )(c.reshape(D_MODEL, 1), w_ada, b_ada.reshape(1, n))


INP_TM = 1024
INP_TN = 1024


def _inproj_kernel(x_ref, sc_ref, sh_ref, w_ref, wg_ref, o_ref, og_ref, u_scr):
    @pl.when(pl.program_id(1) == 0)
    def _():
        u = _layer_norm(x_ref[...]) * (1.0 + sc_ref[...]) + sh_ref[...]
        ub = u.astype(BF16)
        u_scr[...] = ub
        og_ref[...] = _dot(ub, wg_ref[...])

    o_ref[...] = _dot(u_scr[...], w_ref[...])


def _inproj(x, scale, shift, w_main, w_gate):
    t = x.shape[0]
    tm = min(INP_TM, t)
    return pl.pallas_call(
        _inproj_kernel,
        grid=(t // tm, N_MAIN // INP_TN),
        in_specs=[
            pl.BlockSpec((tm, D_MODEL), lambda i, j: (i, 0)),
            pl.BlockSpec((1, D_MODEL), lambda i, j: (0, 0)),
            pl.BlockSpec((1, D_MODEL), lambda i, j: (0, 0)),
            pl.BlockSpec((D_MODEL, INP_TN), lambda i, j: (0, j)),
            pl.BlockSpec((D_MODEL, GATE_PAD), lambda i, j: (0, 0)),
        ],
        out_specs=[
            pl.BlockSpec((tm, INP_TN), lambda i, j: (i, j)),
            pl.BlockSpec((tm, GATE_PAD), lambda i, j: (i, 0)),
        ],
        out_shape=[
            jax.ShapeDtypeStruct((t, N_MAIN), F32),
            jax.ShapeDtypeStruct((t, GATE_PAD), F32),
        ],
        scratch_shapes=[pltpu.VMEM((tm, D_MODEL), BF16)],
        compiler_params=_cparams(("arbitrary", "arbitrary")),
        name="ln_inproj",
    )(x, scale, shift, w_main, w_gate)


GP_TM = 512


def _split3(x):
    hi = x.astype(BF16)
    r1 = x - hi.astype(F32)
    mid = r1.astype(BF16)
    lo = (r1 - mid.astype(F32)).astype(BF16)
    return hi, mid, lo


def _tri_dot(tri, x):
    hi, mid, lo = _split3(x)
    return _dot(tri, hi) + _dot(tri, mid) + _dot(tri, lo)


def _gateprep_kernel(g_ref, bias_ref, alog_ref, o_ref):
    tm = g_ref.shape[0]
    lane = lax.broadcasted_iota(jnp.int32, (CHUNK, GATE_PAD), 1)
    ii = lax.broadcasted_iota(jnp.int32, (CHUNK, CHUNK), 0)
    jj = lax.broadcasted_iota(jnp.int32, (CHUNK, CHUNK), 1)
    tril = jnp.where(jj <= ii, 1.0, 0.0).astype(BF16)
    triu = jnp.where(jj >= ii, 1.0, 0.0).astype(BF16)
    neg_a = -jnp.exp(alog_ref[...])
    is_i = lane < COL_F
    is_f = (lane >= COL_F) & (lane < COL_B)
    is_b = (lane >= COL_B) & (lane < COL_A)
    fwd_cum = ((lane >= COL_F) & (lane < COL_F + M_HEADS)) | ((lane >= COL_A) & (lane < COL_A + G_HEADS))
    bwd_cum = ((lane >= COL_F + M_HEADS) & (lane < COL_B)) | ((lane >= COL_A + G_HEADS) & (lane < N_GATE))
    for c in range(tm // CHUNK):
        rows = slice(c * CHUNK, (c + 1) * CHUNK)
        x = g_ref[rows, :] + bias_ref[...]
        cap = GATE_SOFTCAP * jnp.tanh(x / GATE_SOFTCAP)
        log_f = jnp.minimum(cap, 0.0) - jnp.log1p(jnp.exp(-jnp.abs(cap)))
        beta = _sigmoid(x)
        decay = neg_a * _softplus(x)
        act = jnp.where(is_i, cap, jnp.where(is_f, log_f, jnp.where(is_b, beta, decay)))
        cum_f = _tri_dot(tril, act)
        cum_b = _tri_dot(triu, act)
        o_ref[rows, :] = jnp.where(fwd_cum, cum_f, jnp.where(bwd_cum, cum_b, act))


def _gateprep(graw, bias_row, alog_row):
    t = graw.shape[0]
    tm = min(GP_TM, t)
    return pl.pallas_call(
        _gateprep_kernel,
        grid=(t // tm,),
        in_specs=[
            pl.BlockSpec((tm, GATE_PAD), lambda i: (i, 0)),
            pl.BlockSpec((1, GATE_PAD), lambda i: (0, 0)),
            pl.BlockSpec((1, GATE_PAD), lambda i: (0, 0)),
        ],
        out_specs=pl.BlockSpec((tm, GATE_PAD), lambda i: (i, 0)),
        out_shape=jax.ShapeDtypeStruct((t, GATE_PAD), F32),
        compiler_params=_cparams(("arbitrary",)),
        name="gate_prep",
    )(graw, bias_row, alog_row)


CV_TM = 256
HALO = 8


def _conv_kernel(cur_ref, prev_ref, next_ref, cw_ref, o_ref):
    i = pl.program_id(0)
    n = pl.num_programs(0)
    tm = cur_ref.shape[0]
    pad = CONV_WIDTH // 2
    keep_prev = jnp.where(i > 0, 1.0, 0.0)
    keep_next = jnp.where(i < n - 1, 1.0, 0.0)
    for cb in range(3 * G_HEADS):
        cols = slice(cb * G_HEAD_DIM, (cb + 1) * G_HEAD_DIM)
        xp = jnp.concatenate(
            [prev_ref[:, cols] * keep_prev, cur_ref[:, cols], next_ref[:, cols] * keep_next], axis=0)
        acc = jnp.zeros((tm, G_HEAD_DIM), F32)
        for w in range(CONV_WIDTH):
            lo = HALO - pad + w
            acc = acc + xp[lo:lo + tm, :] * cw_ref[w:w + 1, cols]
        y = _silu(acc)
        if cb < 2 * G_HEADS:
            y = y * lax.rsqrt(jnp.sum(y * y, axis=-1, keepdims=True) + NORM_EPS)
            if cb < G_HEADS:
                y = y * (G_HEAD_DIM ** -0.5)
        o_ref[:, cols] = y


def _gdn_conv(proj, conv_w):
    t = proj.shape[0]
    tm = min(CV_TM, t)
    nh = tm // HALO
    nb = t // HALO
    c3 = 3 * G_W
    cblk = OFF_GQKV // c3
    return pl.pallas_call(
        _conv_kernel,
        grid=(t // tm,),
        in_specs=[
            pl.BlockSpec((tm, c3), lambda i: (i, cblk)),
            pl.BlockSpec((HALO, c3), lambda i: (jnp.maximum(i * nh - 1, 0), cblk)),
            pl.BlockSpec((HALO, c3), lambda i: (jnp.minimum((i + 1) * nh, nb - 1), cblk)),
            pl.BlockSpec((8, c3), lambda i: (0, 0)),
        ],
        out_specs=pl.BlockSpec((tm, c3), lambda i: (i, 0)),
        out_shape=jax.ShapeDtypeStruct((t, c3), F32),
        compiler_params=_cparams(("arbitrary",)),
        name="gdn_conv",
    )(proj, proj, proj, conv_w)


SCAN_RB = 512


def _tri_masks(reverse):
    ii = lax.broadcasted_iota(jnp.int32, (CHUNK, CHUNK), 0)
    jj = lax.broadcasted_iota(jnp.int32, (CHUNK, CHUNK), 1)
    if reverse:
        return jj >= ii, jj > ii
    return jj <= ii, jj < ii


def _mlstm_chunk(q, k, v, ic_c, bc_c, ic_r, bc_r, c_st, n_st, m_st, reverse):
    incl, _ = _tri_masks(reverse)
    d = jnp.where(incl, bc_c - bc_r + ic_r, -jnp.inf)
    inter = bc_c + m_st
    m_row = jnp.maximum(jnp.max(d, axis=1, keepdims=True), inter)
    qb = q.astype(BF16)
    vb = v.astype(BF16)
    w = jnp.exp(d - m_row) * _dot_nt(qb, k.astype(BF16))
    s_inter = jnp.exp(inter - m_row)
    num = _dot(w.astype(BF16), vb) + s_inter * _dot(qb, c_st.astype(BF16))
    den = jnp.sum(w, axis=1, keepdims=True) + s_inter * jnp.sum(q * n_st, axis=1, keepdims=True)
    out = num / jnp.maximum(jnp.abs(den), jnp.exp(-m_row))
    b_last = bc_c[0:1, :] if reverse else bc_c[CHUNK - 1:CHUNK, :]
    w_log = b_last - bc_c + ic_c
    m_new = jnp.maximum(b_last + m_st, jnp.max(w_log, axis=0, keepdims=True))
    carry = jnp.exp(b_last + m_st - m_new)
    kw = k * jnp.exp(w_log - m_new)
    c_new = carry * c_st + _dot_tn(kw.astype(BF16), vb)
    n_new = carry * n_st + jnp.sum(kw, axis=0, keepdims=True)
    return out, c_new, n_new, m_new


def _mlstm_kernel(qf_ref, kf_ref, vf_ref, gcf_ref, grf_ref,
                  qb_ref, kb_ref, vb_ref, gcb_ref, grb_ref,
                  of_ref, ob_ref, c_scr, n_scr, m_scr):
    @pl.when(pl.program_id(0) == 0)
    def _():
        c_scr[...] = jnp.zeros_like(c_scr)
        n_scr[...] = jnp.zeros_like(n_scr)
        m_scr[...] = jnp.zeros_like(m_scr)

    ncb = qf_ref.shape[0] // CHUNK
    kscale = M_QK_DIM ** -0.5

    def body(c, carry):
        for reverse in (False, True):
            cc = (ncb - 1 - c) if reverse else c
            rows = pl.ds(pl.multiple_of(cc * CHUNK, CHUNK), CHUNK)
            q_ref, k_ref, v_ref, gc_ref, gr_ref, o_ref = (
                (qb_ref, kb_ref, vb_ref, gcb_ref, grb_ref, ob_ref) if reverse
                else (qf_ref, kf_ref, vf_ref, gcf_ref, grf_ref, of_ref))
            gc = gc_ref[rows, :]
            gr = gr_ref[cc]
            for h in range(M_HEADS):
                s = (1 if reverse else 0) * M_HEADS + h
                ci, cf = COL_I + s, COL_F + s
                q = q_ref[rows, h * M_QK_DIM:(h + 1) * M_QK_DIM]
                k = k_ref[rows, h * M_QK_DIM:(h + 1) * M_QK_DIM] * kscale
                v = v_ref[rows, h * M_V_DIM:(h + 1) * M_V_DIM]
                out, c_new, n_new, m_new = _mlstm_chunk(
                    q, k, v, gc[:, ci:ci + 1], gc[:, cf:cf + 1], gr[ci:ci + 1, :], gr[cf:cf + 1, :],
                    c_scr[s], n_scr[s], m_scr[s][:, 0:1], reverse)
                o_ref[rows, h * M_V_DIM:(h + 1) * M_V_DIM] = out
                c_scr[s] = c_new
                n_scr[s] = n_new
                m_scr[s] = jnp.broadcast_to(m_new, (1, 128))
        return carry

    lax.fori_loop(0, ncb, body, 0)


def _mlstm(proj, g_col, g_row):
    t = proj.shape[0]
    rb = min(SCAN_RB, t)
    nb = t // rb
    ncb = rb // CHUNK
    fwd = lambda s: s
    bwd = lambda s: nb - 1 - s

    def specs(rmap):
        return [
            pl.BlockSpec((rb, M_QK), lambda s: (rmap(s), OFF_MQ // M_QK)),
            pl.BlockSpec((rb, M_QK), lambda s: (rmap(s), OFF_MK // M_QK)),
            pl.BlockSpec((rb, M_V), lambda s: (rmap(s), OFF_MV // M_V)),
            pl.BlockSpec((rb, GATE_PAD), lambda s: (rmap(s), 0)),
            pl.BlockSpec((ncb, GATE_PAD, CHUNK), lambda s: (rmap(s), 0, 0)),
        ]

    ns = 2 * M_HEADS
    return pl.pallas_call(
        _mlstm_kernel,
        grid=(nb,),
        in_specs=specs(fwd) + specs(bwd),
        out_specs=[pl.BlockSpec((rb, M_V), lambda s: (s, 0)),
                   pl.BlockSpec((rb, M_V), lambda s: (nb - 1 - s, 0))],
        out_shape=[jax.ShapeDtypeStruct((t, M_V), F32)] * 2,
        scratch_shapes=[pltpu.VMEM((ns, M_QK_DIM, M_V_DIM), F32),
                        pltpu.VMEM((ns, 1, M_QK_DIM), F32),
                        pltpu.VMEM((ns, 1, 128), F32)],
        compiler_params=_cparams(("arbitrary",)),
        name="mlstm_scan",
    )(proj, proj, proj, g_col, g_row, proj, proj, proj, g_col, g_row)


def _gdn_chunk(q, k, v, beta_c, gc_c, gc_r, s_st, reverse):
    incl, strict = _tri_masks(reverse)
    decay = jnp.exp(jnp.where(incl, gc_c - gc_r, -jnp.inf))
    kb16 = k.astype(BF16)
    kbeta = k * beta_c
    a = jnp.where(strict, _dot_nt(kbeta.astype(BF16), kb16) * decay, 0.0)
    ii = lax.broadcasted_iota(jnp.int32, (CHUNK, CHUNK), 0)
    jj = lax.broadcasted_iota(jnp.int32, (CHUNK, CHUNK), 1)
    pw = -a
    inv = jnp.where(ii == jj, 1.0, 0.0) + pw
    for _ in range(5):
        pwb = pw.astype(BF16)
        pw = _dot(pwb, pwb)
        inv = inv + _dot(inv.astype(BF16), pw.astype(BF16))
    rhs = jnp.concatenate([v * beta_c, kbeta * jnp.exp(gc_c)], axis=1).astype(BF16)
    sol = _dot(inv.astype(BF16), rhs)
    u = sol[:, :G_HEAD_DIM]
    w = sol[:, G_HEAD_DIM:]
    attn = _dot_nt(q.astype(BF16), kb16) * decay
    sb = s_st.astype(BF16)
    v_new = u - _dot(w.astype(BF16), sb)
    vnb = v_new.astype(BF16)
    out = _dot((q * jnp.exp(gc_c)).astype(BF16), sb) + _dot(attn.astype(BF16), vnb)
    g_last = gc_c[0:1, :] if reverse else gc_c[CHUNK - 1:CHUNK, :]
    s_new = s_st * jnp.exp(g_last) + _dot_tn((k * jnp.exp(g_last - gc_c)).astype(BF16), vnb)
    return out, s_new


def _gdn_kernel(qf_ref, kf_ref, vf_ref, gcf_ref, grf_ref,
                qb_ref, kb_ref, vb_ref, gcb_ref, grb_ref,
                of_ref, ob_ref, s_scr):
    @pl.when(pl.program_id(0) == 0)
    def _():
        s_scr[...] = jnp.zeros_like(s_scr)

    ncb = qf_ref.shape[0] // CHUNK

    def body(c, carry):
        for reverse in (False, True):
            cc = (ncb - 1 - c) if reverse else c
            rows = pl.ds(pl.multiple_of(cc * CHUNK, CHUNK), CHUNK)
            q_ref, k_ref, v_ref, gc_ref, gr_ref, o_ref = (
                (qb_ref, kb_ref, vb_ref, gcb_ref, grb_ref, ob_ref) if reverse
                else (qf_ref, kf_ref, vf_ref, gcf_ref, grf_ref, of_ref))
            gc = gc_ref[rows, :]
            gr = gr_ref[cc]
            for h in range(G_HEADS):
                s = (1 if reverse else 0) * G_HEADS + h
                cb, ca = COL_B + s, COL_A + s
                cols = slice(h * G_HEAD_DIM, (h + 1) * G_HEAD_DIM)
                out, s_new = _gdn_chunk(
                    q_ref[rows, cols], k_ref[rows, cols], v_ref[rows, cols],
                    gc[:, cb:cb + 1], gc[:, ca:ca + 1], gr[ca:ca + 1, :], s_scr[s], reverse)
                o_ref[rows, cols] = out
                s_scr[s] = s_new
        return carry

    lax.fori_loop(0, ncb, body, 0)


def _gdn(qkv, g_col, g_row):
    t = qkv.shape[0]
    rb = min(SCAN_RB, t)
    nb = t // rb
    ncb = rb // CHUNK
    fwd = lambda s: s
    bwd = lambda s: nb - 1 - s

    def specs(rmap):
        return [
            pl.BlockSpec((rb, G_W), lambda s: (rmap(s), 0)),
            pl.BlockSpec((rb, G_W), lambda s: (rmap(s), 1)),
            pl.BlockSpec((rb, G_W), lambda s: (rmap(s), 2)),
            pl.BlockSpec((rb, GATE_PAD), lambda s: (rmap(s), 0)),
            pl.BlockSpec((ncb, GATE_PAD, CHUNK), lambda s: (rmap(s), 0, 0)),
        ]

    return pl.pallas_call(
        _gdn_kernel,
        grid=(nb,),
        in_specs=specs(fwd) + specs(bwd),
        out_specs=[pl.BlockSpec((rb, G_W), lambda s: (s, 0)),
                   pl.BlockSpec((rb, G_W), lambda s: (nb - 1 - s, 0))],
        out_shape=[jax.ShapeDtypeStruct((t, G_W), F32)] * 2,
        scratch_shapes=[pltpu.VMEM((2 * G_HEADS, G_HEAD_DIM, G_HEAD_DIM), F32)],
        compiler_params=_cparams(("arbitrary",)),
        name="gdn_scan",
    )(qkv, qkv, qkv, g_col, g_row, qkv, qkv, qkv, g_col, g_row)


OP_TM = 256


def _outproj_kernel(mf_ref, mb_ref, gf_ref, gb_ref, mo_ref, gz_ref, x_ref, mod_ref,
                    mnw_ref, gnw_ref, wout_ref, ln_ref, rw_ref,
                    h1_ref, u2_ref, lg_ref):
    hm = mf_ref[...] + mb_ref[...]
    hg = gf_ref[...] + gb_ref[...]
    parts = []
    for h in range(M_HEADS):
        seg = hm[:, h * M_V_DIM:(h + 1) * M_V_DIM]
        parts.append(seg * lax.rsqrt(jnp.mean(seg * seg, axis=-1, keepdims=True) + NORM_EPS))
    hm_n = jnp.concatenate(parts, axis=1) * mnw_ref[...] * _sigmoid(mo_ref[...])
    parts = []
    for h in range(G_HEADS):
        seg = hg[:, h * G_HEAD_DIM:(h + 1) * G_HEAD_DIM]
        parts.append(seg * lax.rsqrt(jnp.mean(seg * seg, axis=-1, keepdims=True) + NORM_EPS))
    hg_n = jnp.concatenate(parts, axis=1) * gnw_ref[...] * _silu(gz_ref[...])
    mixed = jnp.concatenate([hm_n, hg_n], axis=1).astype(BF16)
    y = _dot(mixed, wout_ref[...])
    gate1, scale2, shift2 = mod_ref[0:1, :], mod_ref[1:2, :], mod_ref[2:3, :]
    h1 = _layer_norm(DEEPNORM_ALPHA * x_ref[...] + (1.0 + gate1) * y) * ln_ref[0:1, :] + ln_ref[1:2, :]
    h1_ref[...] = h1
    u2 = _layer_norm(h1) * (1.0 + scale2) + shift2
    u2_ref[...] = u2
    lg_ref[...] = lax.dot_general(rw_ref[...], u2, (((1,), (1,)), ((), ())),
                                  precision=lax.Precision.HIGHEST, preferred_element_type=F32)


def _outproj(mf, mb, gf, gb, proj, x, mod3, mnw, gnw, w_out, ln1, rw_t):
    t = x.shape[0]
    tm = min(OP_TM, t)
    row = lambda i: (i, 0)
    const = lambda i: (0, 0)
    return pl.pallas_call(
        _outproj_kernel,
        grid=(t // tm,),
        in_specs=[
            pl.BlockSpec((tm, M_V), row), pl.BlockSpec((tm, M_V), row),
            pl.BlockSpec((tm, G_W), row), pl.BlockSpec((tm, G_W), row),
            pl.BlockSpec((tm, M_V), lambda i: (i, OFF_MO // M_V)),
            pl.BlockSpec((tm, G_W), lambda i: (i, OFF_GZ // G_W)),
            pl.BlockSpec((tm, D_MODEL), row),
            pl.BlockSpec((8, D_MODEL), const),
            pl.BlockSpec((1, M_V), const), pl.BlockSpec((1, G_W), const),
            pl.BlockSpec((D_MODEL, D_MODEL), const),
            pl.BlockSpec((8, D_MODEL), const),
            pl.BlockSpec((N_EXPERTS, D_MODEL), const),
        ],
        out_specs=[
            pl.BlockSpec((tm, D_MODEL), row),
            pl.BlockSpec((tm, D_MODEL), row),
            pl.BlockSpec((N_EXPERTS, tm), lambda i: (0, i)),
        ],
        out_shape=[
            jax.ShapeDtypeStruct((t, D_MODEL), F32),
            jax.ShapeDtypeStruct((t, D_MODEL), F32),
            jax.ShapeDtypeStruct((N_EXPERTS, t), F32),
        ],
        compiler_params=_cparams(("arbitrary",)),
        name="outproj_ln",
    )(mf, mb, gf, gb, proj, proj, x, mod3, mnw, gnw, w_out, ln1, rw_t)


RT_TN = 512
GROUP_SIZE = N_EXPERTS // N_GROUPS


def _route_kernel(lg_ref, bias_ref, idx_ref, w_ref, rank_ref, cnt_ref, carry_scr):
    tn = lg_ref.shape[1]

    @pl.when(pl.program_id(0) == 0)
    def _():
        carry_scr[...] = jnp.zeros_like(carry_scr)

    neg = -jnp.inf
    scores = _sigmoid(lg_ref[...])
    biased = scores + bias_ref[...]
    sub8 = lax.broadcasted_iota(jnp.int32, (GROUP_SIZE, tn), 0).astype(F32)
    gscore = []
    for g in range(N_GROUPS):
        bg = biased[g * GROUP_SIZE:(g + 1) * GROUP_SIZE, :]
        m1 = jnp.max(bg, axis=0, keepdims=True)
        first = jnp.min(jnp.where(bg == m1, sub8, float(GROUP_SIZE)), axis=0, keepdims=True)
        m2 = jnp.max(jnp.where(sub8 == first, neg, bg), axis=0, keepdims=True)
        gscore.append(m1 + m2)
    masked = []
    for g in range(N_GROUPS):
        beaten = jnp.zeros((1, tn), F32)
        for g2 in range(N_GROUPS):
            if g2 == g:
                continue
            wins = (gscore[g2] >= gscore[g]) if g2 < g else (gscore[g2] > gscore[g])
            beaten = beaten + jnp.where(wins, 1.0, 0.0)
        keep = beaten < float(TOPK_GROUPS)
        masked.append(jnp.where(keep, biased[g * GROUP_SIZE:(g + 1) * GROUP_SIZE, :], neg))
    x = jnp.concatenate(masked, axis=0)
    eidx = lax.broadcasted_iota(jnp.int32, (N_EXPERTS, tn), 0).astype(F32)
    member = jnp.zeros((N_EXPERTS, tn), F32)
    sel_idx, sel_s = [], []
    for _ in range(TOP_K):
        m = jnp.max(x, axis=0, keepdims=True)
        idx = jnp.min(jnp.where(x == m, eidx, float(N_EXPERTS)), axis=0, keepdims=True)
        sel = eidx == idx
        sel_idx.append(idx)
        sel_s.append(jnp.sum(jnp.where(sel, scores, 0.0), axis=0, keepdims=True))
        member = member + jnp.where(sel, 1.0, 0.0)
        x = jnp.where(sel, neg, x)
    total = sel_s[0]
    for s in sel_s[1:]:
        total = total + s
    ti = lax.broadcasted_iota(jnp.int32, (tn, tn), 0)
    tj = lax.broadcasted_iota(jnp.int32, (tn, tn), 1)
    before = jnp.where(ti < tj, 1.0, 0.0).astype(BF16)
    carry = carry_scr[:, 0:1]
    excl = _dot(member.astype(BF16), before) + carry
    ranks = [jnp.sum(jnp.where(eidx == i, excl, 0.0), axis=0, keepdims=True) for i in sel_idx]
    zero = jnp.zeros((1, tn), F32)
    idx_ref[...] = jnp.concatenate(sel_idx + [zero, zero], axis=0).astype(jnp.int32)
    w_ref[...] = jnp.concatenate([ROUTED_SCALE * s / total for s in sel_s] + [zero, zero], axis=0)
    rank_ref[...] = jnp.concatenate(ranks + [zero, zero], axis=0).astype(jnp.int32)
    new_carry = carry + jnp.sum(member, axis=1, keepdims=True)
    carry_scr[...] = jnp.broadcast_to(new_carry, carry_scr.shape)
    cnt_ref[...] = jnp.broadcast_to(new_carry, cnt_ref.shape).astype(jnp.int32)


def _route(logits_t, bias_col):
    t = logits_t.shape[1]
    tn = min(RT_TN, t)
    col = lambda i: (0, i)
    return pl.pallas_call(
        _route_kernel,
        grid=(t // tn,),
        in_specs=[pl.BlockSpec((N_EXPERTS, tn), col), pl.BlockSpec((N_EXPERTS, 1), lambda i: (0, 0))],
        out_specs=[pl.BlockSpec((8, tn), col), pl.BlockSpec((8, tn), col), pl.BlockSpec((8, tn), col),
                   pl.BlockSpec((N_EXPERTS, 128), lambda i: (0, 0))],
        out_shape=[jax.ShapeDtypeStruct((8, t), jnp.int32), jax.ShapeDtypeStruct((8, t), F32),
                   jax.ShapeDtypeStruct((8, t), jnp.int32), jax.ShapeDtypeStruct((N_EXPERTS, 128), jnp.int32)],
        scratch_shapes=[pltpu.VMEM((N_EXPERTS, 128), F32)],
        compiler_params=_cparams(("arbitrary",)),
        name="route_topk",
    )(logits_t, bias_col)


def _gmm_kernel(be_ref, nb_ref, x_ref, w_ref, *rest, mode):
    o_ref = rest[-1]
    b = pl.program_id(0)

    @pl.when(b < nb_ref[0])
    def _():
        acc = _dot(x_ref[...].astype(BF16), w_ref[0].astype(BF16))
        if mode == "silu":
            acc = _silu(acc)
        elif mode == "mul":
            acc = acc * rest[0][...].astype(F32)
        elif mode == "rowscale":
            acc = acc * rest[0][...]
        o_ref[...] = acc.astype(o_ref.dtype)

    @pl.when(b >= nb_ref[0])
    def _():
        o_ref[...] = jnp.zeros_like(o_ref)


def _gmm(x, w, block_e, n_used, extra, mode, out_dtype):
    rows, k = x.shape
    n = w.shape[2]
    tb = EXPERT_BLOCK
    in_specs = [
        pl.BlockSpec((tb, k), lambda b, be, nb: (b, 0)),
        pl.BlockSpec((1, k, n), lambda b, be, nb: (be[b], 0, 0)),
    ]
    args = [x, w]
    if mode == "mul":
        in_specs.append(pl.BlockSpec((tb, n), lambda b, be, nb: (b, 0)))
        args.append(extra)
    elif mode == "rowscale":
        in_specs.append(pl.BlockSpec((tb, 1), lambda b, be, nb: (b, 0)))
        args.append(extra)
    return pl.pallas_call(
        functools.partial(_gmm_kernel, mode=mode),
        grid_spec=pltpu.PrefetchScalarGridSpec(
            num_scalar_prefetch=2,
            grid=(rows // tb,),
            in_specs=in_specs,
            out_specs=pl.BlockSpec((tb, n), lambda b, be, nb: (b, 0)),
        ),
        out_shape=jax.ShapeDtypeStruct((rows, n), out_dtype),
        compiler_params=_cparams(("arbitrary",)),
        name="gmm_" + mode,
    )(block_e, n_used, *args)


def _swiglu_grouped(x, wg, wu, wd, block_e, n_used, row_scale):
    a = _gmm(x, wg, block_e, n_used, None, "silu", BF16)
    h = _gmm(x, wu, block_e, n_used, a, "mul", BF16)
    return _gmm(h, wd, block_e, n_used, row_scale, "rowscale", F32)


FN_TM = 512


def _final_kernel(h1_ref, sh_ref, rt_ref, mod_ref, ln_ref, o_ref):
    y = sh_ref[...] + rt_ref[...]
    gate2 = mod_ref[3:4, :]
    o_ref[...] = (_layer_norm(DEEPNORM_ALPHA * h1_ref[...] + (1.0 + gate2) * y) * ln_ref[0:1, :]
                  + ln_ref[1:2, :])


def _final(h1, shared, routed, mod3, ln2):
    t = h1.shape[0]
    tm = min(FN_TM, t)
    row = lambda i: (i, 0)
    const = lambda i: (0, 0)
    return pl.pallas_call(
        _final_kernel,
        grid=(t // tm,),
        in_specs=[pl.BlockSpec((tm, D_MODEL), row)] * 3 + [pl.BlockSpec((8, D_MODEL), const)] * 2,
        out_specs=pl.BlockSpec((tm, D_MODEL), row),
        out_shape=jax.ShapeDtypeStruct((t, D_MODEL), F32),
        compiler_params=_cparams(("arbitrary",)),
        name="final_ln",
    )(h1, shared, routed, mod3, ln2)


def _pad_rows(rows, n=8):
    a = jnp.concatenate(rows, axis=0)
    return jnp.pad(a, ((0, n - a.shape[0]), (0, 0)))


def _mixer(x2, scale1, shift1, w_in, m_igate_bias, m_fgate_bias, g_conv_w, g_A_log, g_dt_bias):
    t = x2.shape[0]
    c0 = 2 * M_QK + 2 * M_V
    c1 = c0 + 4 * M_HEADS
    c2 = c1 + 4 * G_W
    w_main = jnp.concatenate([w_in[:, :c0], w_in[:, c1:c2]], axis=1).astype(BF16)
    w_gate = jnp.pad(jnp.concatenate([w_in[:, c0:c1], w_in[:, c2:]], axis=1),
                     ((0, 0), (0, GATE_PAD - N_GATE))).astype(BF16)
    proj, graw = _inproj(x2, scale1, shift1, w_main, w_gate)
    bias_row = jnp.pad(jnp.concatenate([m_igate_bias.reshape(-1), m_fgate_bias.reshape(-1),
                                        jnp.zeros((2 * G_HEADS,), F32), g_dt_bias.reshape(-1)]),
                       (0, GATE_PAD - N_GATE)).reshape(1, GATE_PAD)
    alog_row = jnp.pad(g_A_log.reshape(-1), (COL_A, GATE_PAD - N_GATE)).reshape(1, GATE_PAD)
    g_col = _gateprep(graw, bias_row, alog_row)
    g_row = jnp.swapaxes(g_col.reshape(t // CHUNK, CHUNK, GATE_PAD), 1, 2)
    mf, mb = _mlstm(proj, g_col, g_row)
    qkv = _gdn_conv(proj, jnp.pad(g_conv_w, ((0, 8 - CONV_WIDTH), (0, 0))))
    gf, gb = _gdn(qkv, g_col, g_row)
    return proj, mf, mb, gf, gb


def kernel(x, c, w_ada, b_ada, w_in, m_igate_bias, m_fgate_bias, m_norm_w, g_conv_w, g_A_log, g_dt_bias,
           g_norm_w, w_out, ln1_w, ln1_b, router_w, router_bias, e_gate, e_up, e_down, s_gate, s_up,
           s_down, ln2_w, ln2_b):
    bsz, t, d = x.shape
    assert bsz == 1 and d == D_MODEL and w_ada.shape[0] == 1
    x2 = x[0]
    mod = _ada_mod(c, w_ada[0], b_ada[0])
    shift1, scale1, gate1, shift2, scale2, gate2 = [mod[:, i * d:(i + 1) * d] for i in range(6)]
    proj, mf, mb, gf, gb = _mixer(x2, scale1, shift1, w_in[0], m_igate_bias[0], m_fgate_bias[0],
                                  g_conv_w[0], g_A_log[0], g_dt_bias[0])
    mod3 = _pad_rows([gate1, scale2, shift2, gate2])
    ln1 = _pad_rows([ln1_w[0][None], ln1_b[0][None]])
    ln2 = _pad_rows([ln2_w[0][None], ln2_b[0][None]])
    h1, u2, logits_t = _outproj(
        mf, mb, gf, gb, proj, x2, mod3, m_norm_w[0][None], jnp.tile(g_norm_w[0], G_HEADS)[None],
        w_out[0].astype(BF16), ln1, router_w[0].T)
    top_idx, top_w, rank, counts = _route(logits_t, router_bias[0][:, None])
    counts = counts[:, 0]
    n_slots = (t * TOP_K + N_EXPERTS * (EXPERT_BLOCK - 1) + EXPERT_BLOCK - 1) // EXPERT_BLOCK * EXPERT_BLOCK
    n_blocks = n_slots // EXPERT_BLOCK
    padded = (counts + EXPERT_BLOCK - 1) // EXPERT_BLOCK * EXPERT_BLOCK
    padded_end = jnp.cumsum(padded)
    dest = (padded_end - padded)[top_idx[:TOP_K]] + rank[:TOP_K]
    tok = jnp.broadcast_to(jnp.arange(t, dtype=jnp.int32)[None], (TOP_K, t))
    slot_tok = jnp.zeros((n_slots,), jnp.int32).at[dest.reshape(-1)].set(tok.reshape(-1))
    slot_w = jnp.zeros((n_slots,), F32).at[dest.reshape(-1)].set(top_w[:TOP_K].reshape(-1))
    block_e = jnp.minimum(jnp.searchsorted(padded_end, jnp.arange(n_blocks) * EXPERT_BLOCK, side='right'),
                          N_EXPERTS - 1).astype(jnp.int32)
    n_used = (padded_end[-1:] // EXPERT_BLOCK).astype(jnp.int32)
    xs = u2[slot_tok]
    ys = _swiglu_grouped(xs, e_gate[0], e_up[0], e_down[0], block_e, n_used, slot_w[:, None])
    routed = jnp.sum(ys[dest.T], axis=1)
    shared = _swiglu_grouped(u2, s_gate, s_up, s_down, jnp.zeros((t // EXPERT_BLOCK,), jnp.int32),
                             jnp.full((1,), t // EXPERT_BLOCK, jnp.int32), jnp.ones((t, 1), F32))
    out = _final(h1, shared, routed, mod3, ln2)
    return out[None]
```

```python
import functools

import jax
import jax.numpy as jnp
from jax import lax
from jax.experimental import pallas as pl
from jax.experimental.pallas import tpu as pltpu

F32 = jnp.float32
BF16 = jnp.bfloat16

D_MODEL = 2048
M_HEADS = 4
M_QK_DIM = 128
M_V_DIM = 256
G_HEADS = 8
G_HEAD_DIM = 128
CONV_WIDTH = 5
CHUNK = 64
GATE_SOFTCAP = 15.0
N_EXPERTS = 64
TOP_K = 6
N_GROUPS = 8
TOPK_GROUPS = 4
D_EXPERT = 1408
ROUTED_SCALE = 2.5
EXPERT_BLOCK = 256
NORM_EPS = 1e-6
DEEPNORM_ALPHA = 2.0 ** 0.25

M_QK = M_HEADS * M_QK_DIM
M_V = M_HEADS * M_V_DIM
G_W = G_HEADS * G_HEAD_DIM
N_MAIN = 2 * M_QK + 2 * M_V + 3 * G_W + G_W
N_GATE = 2 * M_HEADS + 2 * M_HEADS + 2 * G_HEADS + 2 * G_HEADS
GATE_PAD = 128
COL_I = 0
COL_F = 8
COL_B = 16
COL_A = 32
OFF_MQ, OFF_MK, OFF_MV, OFF_MO = 0, 512, 1024, 2048
OFF_GQKV, OFF_GZ = 3072, 6144

VMEM_LIMIT = 56 * 1024 * 1024


def _cparams(sem):
    return pltpu.CompilerParams(dimension_semantics=sem, vmem_limit_bytes=VMEM_LIMIT)


def _dot(a, b):
    return jnp.dot(a, b, preferred_element_type=F32)


def _dot_nt(a, b):
    return lax.dot_general(a, b, (((1,), (1,)), ((), ())), preferred_element_type=F32)


def _dot_tn(a, b):
    return lax.dot_general(a, b, (((0,), (0,)), ((), ())), preferred_element_type=F32)


def _layer_norm(x):
    mu = jnp.mean(x, axis=-1, keepdims=True)
    xc = x - mu
    var = jnp.mean(xc * xc, axis=-1, keepdims=True)
    return xc * lax.rsqrt(var + NORM_EPS)


def _sigmoid(x):
    return 1.0 / (1.0 + jnp.exp(-x))


def _silu(x):
    return x * _sigmoid(x)


def _softplus(x):
    return jnp.maximum(x, 0.0) + jnp.log1p(jnp.exp(-jnp.abs(x)))


ADA_TN = 1024
ADA_RC = 64


def _ada_kernel(c_ref, w_ref, b_ref, o_ref):
    def body(r, acc):
        rows = pl.ds(pl.multiple_of(r * ADA_RC, ADA_RC), ADA_RC)
        cond = _silu(c_ref[rows, :])
        blk = w_ref[rows, :] * cond
        return acc + jnp.sum(blk.reshape(ADA_RC // 8, 8, ADA_TN), axis=0)

    acc = lax.fori_loop(0, D_MODEL // ADA_RC, body, jnp.zeros((8, ADA_TN), F32))
    o_ref[...] = jnp.sum(acc, axis=0, keepdims=True) + b_ref[...]


def _ada_mod(c, w_ada, b_ada):
    n = w_ada.shape[1]
    return pl.pallas_call(
        _ada_kernel,
        grid=(n // ADA_TN,),
        in_specs=[
            pl.BlockSpec((D_MODEL, 1), lambda j: (0, 0)),
            pl.BlockSpec((D_MODEL, ADA_TN), lambda j: (0, j)),
            pl.BlockSpec((1, ADA_TN), lambda j: (0, j)),
        ],
        out_specs=pl.BlockSpec((1, ADA_TN), lambda j: (0, j)),
        out_shape=jax.ShapeDtypeStruct((1, n), F32),
        compiler_params=_cparams(("arbitrary",)),
        name="ada_mod",
    )(c.reshape(D_MODEL, 1), w_ada, b_ada.reshape(1, n))


INP_TM = 1024
INP_TN = 1024


def _inproj_kernel(x_ref, sc_ref, sh_ref, w_ref, wg_ref, o_ref, og_ref, u_scr):
    @pl.when(pl.program_id(1) == 0)
    def _():
        u = _layer_norm(x_ref[...]) * (1.0 + sc_ref[...]) + sh_ref[...]
        ub = u.astype(BF16)
        u_scr[...] = ub
        og_ref[...] = _dot(ub, wg_ref[...])

    o_ref[...] = _dot(u_scr[...], w_ref[...])


def _inproj(x, scale, shift, w_main, w_gate):
    t = x.shape[0]
    tm = min(INP_TM, t)
    return pl.pallas_call(
        _inproj_kernel,
        grid=(t // tm, N_MAIN // INP_TN),
        in_specs=[
            pl.BlockSpec((tm, D_MODEL), lambda i, j: (i, 0)),
            pl.BlockSpec((1, D_MODEL), lambda i, j: (0, 0)),
            pl.BlockSpec((1, D_MODEL), lambda i, j: (0, 0)),
            pl.BlockSpec((D_MODEL, INP_TN), lambda i, j: (0, j)),
            pl.BlockSpec((D_MODEL, GATE_PAD), lambda i, j: (0, 0)),
        ],
        out_specs=[
            pl.BlockSpec((tm, INP_TN), lambda i, j: (i, j)),
            pl.BlockSpec((tm, GATE_PAD), lambda i, j: (i, 0)),
        ],
        out_shape=[
            jax.ShapeDtypeStruct((t, N_MAIN), F32),
            jax.ShapeDtypeStruct((t, GATE_PAD), F32),
        ],
        scratch_shapes=[pltpu.VMEM((tm, D_MODEL), BF16)],
        compiler_params=_cparams(("arbitrary", "arbitrary")),
        name="ln_inproj",
    )(x, scale, shift, w_main, w_gate)


GP_TM = 512


def _split3(x):
    hi = x.astype(BF16)
    r1 = x - hi.astype(F32)
    mid = r1.astype(BF16)
    lo = (r1 - mid.astype(F32)).astype(BF16)
    return hi, mid, lo


def _tri_dot(tri, x):
    hi, mid, lo = _split3(x)
    return _dot(tri, hi) + _dot(tri, mid) + _dot(tri, lo)


def _gateprep_kernel(g_ref, bias_ref, alog_ref, o_ref):
    tm = g_ref.shape[0]
    lane = lax.broadcasted_iota(jnp.int32, (CHUNK, GATE_PAD), 1)
    ii = lax.broadcasted_iota(jnp.int32, (CHUNK, CHUNK), 0)
    jj = lax.broadcasted_iota(jnp.int32, (CHUNK, CHUNK), 1)
    tril = jnp.where(jj <= ii, 1.0, 0.0).astype(BF16)
    triu = jnp.where(jj >= ii, 1.0, 0.0).astype(BF16)
    neg_a = -jnp.exp(alog_ref[...])
    is_i = lane < COL_F
    is_f = (lane >= COL_F) & (lane < COL_B)
    is_b = (lane >= COL_B) & (lane < COL_A)
    fwd_cum = ((lane >= COL_F) & (lane < COL_F + M_HEADS)) | ((lane >= COL_A) & (lane < COL_A + G_HEADS))
    bwd_cum = ((lane >= COL_F + M_HEADS) & (lane < COL_B)) | ((lane >= COL_A + G_HEADS) & (lane < N_GATE))
    for c in range(tm // CHUNK):
        rows = slice(c * CHUNK, (c + 1) * CHUNK)
        x = g_ref[rows, :] + bias_ref[...]
        cap = GATE_SOFTCAP * jnp.tanh(x / GATE_SOFTCAP)
        log_f = jnp.minimum(cap, 0.0) - jnp.log1p(jnp.exp(-jnp.abs(cap)))
        beta = _sigmoid(x)
        decay = neg_a * _softplus(x)
        act = jnp.where(is_i, cap, jnp.where(is_f, log_f, jnp.where(is_b, beta, decay)))
        cum_f = _tri_dot(tril, act)
        cum_b = _tri_dot(triu, act)
        o_ref[rows, :] = jnp.where(fwd_cum, cum_f, jnp.where(bwd_cum, cum_b, act))


def _gateprep(graw, bias_row, alog_row):
    t = graw.shape[0]
    tm = min(GP_TM, t)
    return pl.pallas_call(
        _gateprep_kernel,
        grid=(t // tm,),
        in_specs=[
            pl.BlockSpec((tm, GATE_PAD), lambda i: (i, 0)),
            pl.BlockSpec((1, GATE_PAD), lambda i: (0, 0)),
            pl.BlockSpec((1, GATE_PAD), lambda i: (0, 0)),
        ],
        out_specs=pl.BlockSpec((tm, GATE_PAD), lambda i: (i, 0)),
        out_shape=jax.ShapeDtypeStruct((t, GATE_PAD), F32),
        compiler_params=_cparams(("arbitrary",)),
        name="gate_prep",
    )(graw, bias_row, alog_row)


CV_TM = 256
HALO = 8


def _conv_kernel(cur_ref, prev_ref, next_ref, cw_ref, o_ref):
    i = pl.program_id(0)
    n = pl.num_programs(0)
    tm = cur_ref.shape[0]
    pad = CONV_WIDTH // 2
    keep_prev = jnp.where(i > 0, 1.0, 0.0)
    keep_next = jnp.where(i < n - 1, 1.0, 0.0)
    for cb in range(3 * G_HEADS):
        cols = slice(cb * G_HEAD_DIM, (cb + 1) * G_HEAD_DIM)
        xp = jnp.concatenate(
            [prev_ref[:, cols] * keep_prev, cur_ref[:, cols], next_ref[:, cols] * keep_next], axis=0)
        acc = jnp.zeros((tm, G_HEAD_DIM), F32)
        for w in range(CONV_WIDTH):
            lo = HALO - pad + w
            acc = acc + xp[lo:lo + tm, :] * cw_ref[w:w + 1, cols]
        y = _silu(acc)
        if cb < 2 * G_HEADS:
            y = y * lax.rsqrt(jnp.sum(y * y, axis=-1, keepdims=True) + NORM_EPS)
            if cb < G_HEADS:
                y = y * (G_HEAD_DIM ** -0.5)
        o_ref[:, cols] = y


def _gdn_conv(proj, conv_w):
    t = proj.shape[0]
    tm = min(CV_TM, t)
    nh = tm // HALO
    nb = t // HALO
    c3 = 3 * G_W
    cblk = OFF_GQKV // c3
    return pl.pallas_call(
        _conv_kernel,
        grid=(t // tm,),
        in_specs=[
            pl.BlockSpec((tm, c3), lambda i: (i, cblk)),
            pl.BlockSpec((HALO, c3), lambda i: (jnp.maximum(i * nh - 1, 0), cblk)),
            pl.BlockSpec((HALO, c3), lambda i: (jnp.minimum((i + 1) * nh, nb - 1), cblk)),
            pl.BlockSpec((8, c3), lambda i: (0, 0)),
        ],
        out_specs=pl.BlockSpec((tm, c3), lambda i: (i, 0)),
        out_shape=jax.ShapeDtypeStruct((t, c3), F32),
        compiler_params=_cparams(("arbitrary",)),
        name="gdn_conv",
    )(proj, proj, proj, conv_w)


SCAN_RB = 512


def _tri_masks(reverse):
    ii = lax.broadcasted_iota(jnp.int32, (CHUNK, CHUNK), 0)
    jj = lax.broadcasted_iota(jnp.int32, (CHUNK, CHUNK), 1)
    if reverse:
        return jj >= ii, jj > ii
    return jj <= ii, jj < ii


def _mlstm_chunks(chains):
    for ch in chains:
        incl, _ = _tri_masks(ch["reverse"])
        ch["d"] = jnp.where(incl, ch["bc_c"] - ch["bc_r"] + ch["ic_r"], -jnp.inf)
        ch["inter"] = ch["bc_c"] + ch["m"]
        ch["qb"] = ch["q"].astype(BF16)
        ch["vb"] = ch["v"].astype(BF16)
    for ch in chains:
        ch["qk"] = _dot_nt(ch["qb"], ch["k"].astype(BF16))
    for ch in chains:
        ch["rmax"] = jnp.max(ch["d"], axis=1, keepdims=True)
    for ch in chains:
        ch["qc"] = _dot(ch["qb"], ch["c"].astype(BF16))
    for ch in chains:
        ch["qn"] = jnp.sum(ch["q"] * ch["n"], axis=1, keepdims=True)
    for ch in chains:
        bc_c = ch["bc_c"]
        b_last = bc_c[0:1, :] if ch["reverse"] else bc_c[CHUNK - 1:CHUNK, :]
        w_log = b_last - bc_c + ch["ic_c"]
        ch["m_new"] = jnp.maximum(b_last + ch["m"], jnp.max(w_log, axis=0, keepdims=True))
        ch["carry"] = jnp.exp(b_last + ch["m"] - ch["m_new"])
        ch["kw"] = ch["k"] * jnp.exp(w_log - ch["m_new"])
    for ch in chains:
        ch["kv"] = _dot_tn(ch["kw"].astype(BF16), ch["vb"])
    for ch in chains:
        ch["m_row"] = jnp.maximum(ch["rmax"], ch["inter"])
        ch["s_inter"] = jnp.exp(ch["inter"] - ch["m_row"])
        ch["w"] = jnp.exp(ch["d"] - ch["m_row"]) * ch["qk"]
    for ch in chains:
        ch["wv"] = _dot(ch["w"].astype(BF16), ch["vb"])
    for ch in chains:
        ch["wsum"] = jnp.sum(ch["w"], axis=1, keepdims=True)
    for ch in chains:
        ch["c_new"] = ch["carry"] * ch["c"] + ch["kv"]
        ch["n_new"] = ch["carry"] * ch["n"] + jnp.sum(ch["kw"], axis=0, keepdims=True)
    for ch in chains:
        num = ch["wv"] + ch["s_inter"] * ch["qc"]
        den = ch["wsum"] + ch["s_inter"] * ch["qn"]
        ch["out"] = num / jnp.maximum(jnp.abs(den), jnp.exp(-ch["m_row"]))


def _mlstm_kernel(qf_ref, kf_ref, vf_ref, gcf_ref, grf_ref,
                  qb_ref, kb_ref, vb_ref, gcb_ref, grb_ref,
                  of_ref, ob_ref, c_scr, n_scr, m_scr):
    @pl.when(pl.program_id(0) == 0)
    def _():
        c_scr[...] = jnp.zeros_like(c_scr)
        n_scr[...] = jnp.zeros_like(n_scr)
        m_scr[...] = jnp.zeros_like(m_scr)

    ncb = qf_ref.shape[0] // CHUNK
    kscale = M_QK_DIM ** -0.5

    def body(c, carry):
        chains = []
        for reverse in (False, True):
            cc = (ncb - 1 - c) if reverse else c
            rows = pl.ds(pl.multiple_of(cc * CHUNK, CHUNK), CHUNK)
            q_ref, k_ref, v_ref, gc_ref, gr_ref, o_ref = (
                (qb_ref, kb_ref, vb_ref, gcb_ref, grb_ref, ob_ref) if reverse
                else (qf_ref, kf_ref, vf_ref, gcf_ref, grf_ref, of_ref))
            gc = gc_ref[rows, :]
            gr = gr_ref[cc]
            for h in range(M_HEADS):
                s = (1 if reverse else 0) * M_HEADS + h
                ci, cf = COL_I + s, COL_F + s
                chains.append(dict(
                    reverse=reverse, s=s, o_ref=o_ref, rows=rows, h=h,
                    q=q_ref[rows, h * M_QK_DIM:(h + 1) * M_QK_DIM],
                    k=k_ref[rows, h * M_QK_DIM:(h + 1) * M_QK_DIM] * kscale,
                    v=v_ref[rows, h * M_V_DIM:(h + 1) * M_V_DIM],
                    ic_c=gc[:, ci:ci + 1], bc_c=gc[:, cf:cf + 1],
                    ic_r=gr[ci:ci + 1, :], bc_r=gr[cf:cf + 1, :],
                    c=c_scr[s], n=n_scr[s], m=m_scr[s][:, 0:1]))
        _mlstm_chunks(chains)
        for ch in chains:
            h, s = ch["h"], ch["s"]
            ch["o_ref"][ch["rows"], h * M_V_DIM:(h + 1) * M_V_DIM] = ch["out"]
            c_scr[s] = ch["c_new"]
            n_scr[s] = ch["n_new"]
            m_scr[s] = jnp.broadcast_to(ch["m_new"], (1, 128))
        return carry

    lax.fori_loop(0, ncb, body, 0)


def _mlstm(proj, g_col, g_row):
    t = proj.shape[0]
    rb = min(SCAN_RB, t)
    nb = t // rb
    ncb = rb // CHUNK
    fwd = lambda s: s
    bwd = lambda s: nb - 1 - s

    def specs(rmap):
        return [
            pl.BlockSpec((rb, M_QK), lambda s: (rmap(s), OFF_MQ // M_QK)),
            pl.BlockSpec((rb, M_QK), lambda s: (rmap(s), OFF_MK // M_QK)),
            pl.BlockSpec((rb, M_V), lambda s: (rmap(s), OFF_MV // M_V)),
            pl.BlockSpec((rb, GATE_PAD), lambda s: (rmap(s), 0)),
            pl.BlockSpec((ncb, GATE_PAD, CHUNK), lambda s: (rmap(s), 0, 0)),
        ]

    ns = 2 * M_HEADS
    return pl.pallas_call(
        _mlstm_kernel,
        grid=(nb,),
        in_specs=specs(fwd) + specs(bwd),
        out_specs=[pl.BlockSpec((rb, M_V), lambda s: (s, 0)),
                   pl.BlockSpec((rb, M_V), lambda s: (nb - 1 - s, 0))],
        out_shape=[jax.ShapeDtypeStruct((t, M_V), F32)] * 2,
        scratch_shapes=[pltpu.VMEM((ns, M_QK_DIM, M_V_DIM), F32),
                        pltpu.VMEM((ns, 1, M_QK_DIM), F32),
                        pltpu.VMEM((ns, 1, 128), F32)],
        compiler_params=_cparams(("arbitrary",)),
        name="mlstm_scan",
    )(proj, proj, proj, g_col, g_row, proj, proj, proj, g_col, g_row)


def _gdn_chunks(chains):
    ii = lax.broadcasted_iota(jnp.int32, (CHUNK, CHUNK), 0)
    jj = lax.broadcasted_iota(jnp.int32, (CHUNK, CHUNK), 1)
    eye = jnp.where(ii == jj, 1.0, 0.0)
    for ch in chains:
        incl, strict = _tri_masks(ch["reverse"])
        gc_c = ch["gc_c"]
        ch["strict"] = strict
        ch["decay"] = jnp.exp(jnp.where(incl, gc_c - ch["gc_r"], -jnp.inf))
        ch["kb16"] = ch["k"].astype(BF16)
        ch["kbeta"] = ch["k"] * ch["beta_c"]
        ch["sb"] = ch["s"].astype(BF16)
        ch["g_last"] = gc_c[0:1, :] if ch["reverse"] else gc_c[CHUNK - 1:CHUNK, :]
    for ch in chains:
        ch["kk"] = _dot_nt(ch["kbeta"].astype(BF16), ch["kb16"])
    for ch in chains:
        ch["qk"] = _dot_nt(ch["q"].astype(BF16), ch["kb16"])
    for ch in chains:
        ch["qs"] = _dot((ch["q"] * jnp.exp(ch["gc_c"])).astype(BF16), ch["sb"])
    for ch in chains:
        ch["pw"] = -jnp.where(ch["strict"], ch["kk"] * ch["decay"], 0.0)
        ch["inv"] = eye + ch["pw"]
    for _ in range(5):
        for ch in chains:
            pwb = ch["pw"].astype(BF16)
            ch["pw"] = _dot(pwb, pwb)
        for ch in chains:
            ch["inv"] = ch["inv"] + _dot(ch["inv"].astype(BF16), ch["pw"].astype(BF16))
    for ch in chains:
        rhs = jnp.concatenate([ch["v"] * ch["beta_c"], ch["kbeta"] * jnp.exp(ch["gc_c"])], axis=1)
        ch["sol"] = _dot(ch["inv"].astype(BF16), rhs.astype(BF16))
    for ch in chains:
        w = ch["sol"][:, G_HEAD_DIM:]
        ch["v_new"] = ch["sol"][:, :G_HEAD_DIM] - _dot(w.astype(BF16), ch["sb"])
        ch["vnb"] = ch["v_new"].astype(BF16)
    for ch in chains:
        kdec = ch["k"] * jnp.exp(ch["g_last"] - ch["gc_c"])
        ch["s_new"] = ch["s"] * jnp.exp(ch["g_last"]) + _dot_tn(kdec.astype(BF16), ch["vnb"])
    for ch in chains:
        attn = ch["qk"] * ch["decay"]
        ch["out"] = ch["qs"] + _dot(attn.astype(BF16), ch["vnb"])


def _gdn_kernel(qf_ref, kf_ref, vf_ref, gcf_ref, grf_ref,
                qb_ref, kb_ref, vb_ref, gcb_ref, grb_ref,
                of_ref, ob_ref, s_scr):
    @pl.when(pl.program_id(0) == 0)
    def _():
        s_scr[...] = jnp.zeros_like(s_scr)

    ncb = qf_ref.shape[0] // CHUNK

    def body(c, carry):
        chains = []
        for reverse in (False, True):
            cc = (ncb - 1 - c) if reverse else c
            rows = pl.ds(pl.multiple_of(cc * CHUNK, CHUNK), CHUNK)
            q_ref, k_ref, v_ref, gc_ref, gr_ref, o_ref = (
                (qb_ref, kb_ref, vb_ref, gcb_ref, grb_ref, ob_ref) if reverse
                else (qf_ref, kf_ref, vf_ref, gcf_ref, grf_ref, of_ref))
            gc = gc_ref[rows, :]
            gr = gr_ref[cc]
            for h in range(G_HEADS):
                s = (1 if reverse else 0) * G_HEADS + h
                cb, ca = COL_B + s, COL_A + s
                cols = slice(h * G_HEAD_DIM, (h + 1) * G_HEAD_DIM)
                chains.append(dict(
                    reverse=reverse, slot=s, o_ref=o_ref, rows=rows, cols=cols,
                    q=q_ref[rows, cols], k=k_ref[rows, cols], v=v_ref[rows, cols],
                    beta_c=gc[:, cb:cb + 1], gc_c=gc[:, ca:ca + 1], gc_r=gr[ca:ca + 1, :],
                    s=s_scr[s]))
        _gdn_chunks(chains)
        for ch in chains:
            ch["o_ref"][ch["rows"], ch["cols"]] = ch["out"]
            s_scr[ch["slot"]] = ch["s_new"]
        return carry

    lax.fori_loop(0, ncb, body, 0)


def _gdn(qkv, g_col, g_row):
    t = qkv.shape[0]
    rb = min(SCAN_RB, t)
    nb = t // rb
    ncb = rb // CHUNK
    fwd = lambda s: s
    bwd = lambda s: nb - 1 - s

    def specs(rmap):
        return [
            pl.BlockSpec((rb, G_W), lambda s: (rmap(s), 0)),
            pl.BlockSpec((rb, G_W), lambda s: (rmap(s), 1)),
            pl.BlockSpec((rb, G_W), lambda s: (rmap(s), 2)),
            pl.BlockSpec((rb, GATE_PAD), lambda s: (rmap(s), 0)),
            pl.BlockSpec((ncb, GATE_PAD, CHUNK), lambda s: (rmap(s), 0, 0)),
        ]

    return pl.pallas_call(
        _gdn_kernel,
        grid=(nb,),
        in_specs=specs(fwd) + specs(bwd),
        out_specs=[pl.BlockSpec((rb, G_W), lambda s: (s, 0)),
                   pl.BlockSpec((rb, G_W), lambda s: (nb - 1 - s, 0))],
        out_shape=[jax.ShapeDtypeStruct((t, G_W), F32)] * 2,
        scratch_shapes=[pltpu.VMEM((2 * G_HEADS, G_HEAD_DIM, G_HEAD_DIM), F32)],
        compiler_params=_cparams(("arbitrary",)),
        name="gdn_scan",
    )(qkv, qkv, qkv, g_col, g_row, qkv, qkv, qkv, g_col, g_row)


OP_TM = 256


def _outproj_kernel(mf_ref, mb_ref, gf_ref, gb_ref, mo_ref, gz_ref, x_ref, mod_ref,
                    mnw_ref, gnw_ref, wout_ref, ln_ref, rw_ref,
                    h1_ref, u2_ref, lg_ref):
    hm = mf_ref[...] + mb_ref[...]
    hg = gf_ref[...] + gb_ref[...]
    parts = []
    for h in range(M_HEADS):
        seg = hm[:, h * M_V_DIM:(h + 1) * M_V_DIM]
        parts.append(seg * lax.rsqrt(jnp.mean(seg * seg, axis=-1, keepdims=True) + NORM_EPS))
    hm_n = jnp.concatenate(parts, axis=1) * mnw_ref[...] * _sigmoid(mo_ref[...])
    parts = []
    for h in range(G_HEADS):
        seg = hg[:, h * G_HEAD_DIM:(h + 1) * G_HEAD_DIM]
        parts.append(seg * lax.rsqrt(jnp.mean(seg * seg, axis=-1, keepdims=True) + NORM_EPS))
    hg_n = jnp.concatenate(parts, axis=1) * gnw_ref[...] * _silu(gz_ref[...])
    mixed = jnp.concatenate([hm_n, hg_n], axis=1).astype(BF16)
    y = _dot(mixed, wout_ref[...])
    gate1, scale2, shift2 = mod_ref[0:1, :], mod_ref[1:2, :], mod_ref[2:3, :]
    h1 = _layer_norm(DEEPNORM_ALPHA * x_ref[...] + (1.0 + gate1) * y) * ln_ref[0:1, :] + ln_ref[1:2, :]
    h1_ref[...] = h1
    u2 = _layer_norm(h1) * (1.0 + scale2) + shift2
    u2_ref[...] = u2
    lg_ref[...] = lax.dot_general(rw_ref[...], u2, (((1,), (1,)), ((), ())),
                                  precision=lax.Precision.HIGHEST, preferred_element_type=F32)


def _outproj(mf, mb, gf, gb, proj, x, mod3, mnw, gnw, w_out, ln1, rw_t):
    t = x.shape[0]
    tm = min(OP_TM, t)
    row = lambda i: (i, 0)
    const = lambda i: (0, 0)
    return pl.pallas_call(
        _outproj_kernel,
        grid=(t // tm,),
        in_specs=[
            pl.BlockSpec((tm, M_V), row), pl.BlockSpec((tm, M_V), row),
            pl.BlockSpec((tm, G_W), row), pl.BlockSpec((tm, G_W), row),
            pl.BlockSpec((tm, M_V), lambda i: (i, OFF_MO // M_V)),
            pl.BlockSpec((tm, G_W), lambda i: (i, OFF_GZ // G_W)),
            pl.BlockSpec((tm, D_MODEL), row),
            pl.BlockSpec((8, D_MODEL), const),
            pl.BlockSpec((1, M_V), const), pl.BlockSpec((1, G_W), const),
            pl.BlockSpec((D_MODEL, D_MODEL), const),
            pl.BlockSpec((8, D_MODEL), const),
            pl.BlockSpec((N_EXPERTS, D_MODEL), const),
        ],
        out_specs=[
            pl.BlockSpec((tm, D_MODEL), row),
            pl.BlockSpec((tm, D_MODEL), row),
            pl.BlockSpec((N_EXPERTS, tm), lambda i: (0, i)),
        ],
        out_shape=[
            jax.ShapeDtypeStruct((t, D_MODEL), F32),
            jax.ShapeDtypeStruct((t, D_MODEL), F32),
            jax.ShapeDtypeStruct((N_EXPERTS, t), F32),
        ],
        compiler_params=_cparams(("arbitrary",)),
        name="outproj_ln",
    )(mf, mb, gf, gb, proj, proj, x, mod3, mnw, gnw, w_out, ln1, rw_t)


RT_TN = 512
GROUP_SIZE = N_EXPERTS // N_GROUPS


def _route_kernel(lg_ref, bias_ref, idx_ref, w_ref, rank_ref, cnt_ref, carry_scr):
    tn = lg_ref.shape[1]

    @pl.when(pl.program_id(0) == 0)
    def _():
        carry_scr[...] = jnp.zeros_like(carry_scr)

    neg = -jnp.inf
    scores = _sigmoid(lg_ref[...])
    biased = scores + bias_ref[...]
    sub8 = lax.broadcasted_iota(jnp.int32, (GROUP_SIZE, tn), 0).astype(F32)
    gscore = []
    for g in range(N_GROUPS):
        bg = biased[g * GROUP_SIZE:(g + 1) * GROUP_SIZE, :]
        m1 = jnp.max(bg, axis=0, keepdims=True)
        first = jnp.min(jnp.where(bg == m1, sub8, float(GROUP_SIZE)), axis=0, keepdims=True)
        m2 = jnp.max(jnp.where(sub8 == first, neg, bg), axis=0, keepdims=True)
        gscore.append(m1 + m2)
    masked = []
    for g in range(N_GROUPS):
        beaten = jnp.zeros((1, tn), F32)
        for g2 in range(N_GROUPS):
            if g2 == g:
                continue
            wins = (gscore[g2] >= gscore[g]) if g2 < g else (gscore[g2] > gscore[g])
            beaten = beaten + jnp.where(wins, 1.0, 0.0)
        keep = beaten < float(TOPK_GROUPS)
        masked.append(jnp.where(keep, biased[g * GROUP_SIZE:(g + 1) * GROUP_SIZE, :], neg))
    x = jnp.concatenate(masked, axis=0)
    eidx = lax.broadcasted_iota(jnp.int32, (N_EXPERTS, tn), 0).astype(F32)
    member = jnp.zeros((N_EXPERTS, tn), F32)
    sel_idx, sel_s = [], []
    for _ in range(TOP_K):
        m = jnp.max(x, axis=0, keepdims=True)
        idx = jnp.min(jnp.where(x == m, eidx, float(N_EXPERTS)), axis=0, keepdims=True)
        sel = eidx == idx
        sel_idx.append(idx)
        sel_s.append(jnp.sum(jnp.where(sel, scores, 0.0), axis=0, keepdims=True))
        member = member + jnp.where(sel, 1.0, 0.0)
        x = jnp.where(sel, neg, x)
    total = sel_s[0]
    for s in sel_s[1:]:
        total = total + s
    ti = lax.broadcasted_iota(jnp.int32, (tn, tn), 0)
    tj = lax.broadcasted_iota(jnp.int32, (tn, tn), 1)
    before = jnp.where(ti < tj, 1.0, 0.0).astype(BF16)
    carry = carry_scr[:, 0:1]
    excl = _dot(member.astype(BF16), before) + carry
    ranks = [jnp.sum(jnp.where(eidx == i, excl, 0.0), axis=0, keepdims=True) for i in sel_idx]
    zero = jnp.zeros((1, tn), F32)
    idx_ref[...] = jnp.concatenate(sel_idx + [zero, zero], axis=0).astype(jnp.int32)
    w_ref[...] = jnp.concatenate([ROUTED_SCALE * s / total for s in sel_s] + [zero, zero], axis=0)
    rank_ref[...] = jnp.concatenate(ranks + [zero, zero], axis=0).astype(jnp.int32)
    new_carry = carry + jnp.sum(member, axis=1, keepdims=True)
    carry_scr[...] = jnp.broadcast_to(new_carry, carry_scr.shape)
    cnt_ref[...] = jnp.broadcast_to(new_carry, cnt_ref.shape).astype(jnp.int32)


def _route(logits_t, bias_col):
    t = logits_t.shape[1]
    tn = min(RT_TN, t)
    col = lambda i: (0, i)
    return pl.pallas_call(
        _route_kernel,
        grid=(t // tn,),
        in_specs=[pl.BlockSpec((N_EXPERTS, tn), col), pl.BlockSpec((N_EXPERTS, 1), lambda i: (0, 0))],
        out_specs=[pl.BlockSpec((8, tn), col), pl.BlockSpec((8, tn), col), pl.BlockSpec((8, tn), col),
                   pl.BlockSpec((N_EXPERTS, 128), lambda i: (0, 0))],
        out_shape=[jax.ShapeDtypeStruct((8, t), jnp.int32), jax.ShapeDtypeStruct((8, t), F32),
                   jax.ShapeDtypeStruct((8, t), jnp.int32), jax.ShapeDtypeStruct((N_EXPERTS, 128), jnp.int32)],
        scratch_shapes=[pltpu.VMEM((N_EXPERTS, 128), F32)],
        compiler_params=_cparams(("arbitrary",)),
        name="route_topk",
    )(logits_t, bias_col)


def _gmm_kernel(be_ref, nb_ref, x_ref, w_ref, *rest, mode):
    o_ref = rest[-1]
    b = pl.program_id(0)

    @pl.when(b < nb_ref[0])
    def _():
        acc = _dot(x_ref[...].astype(BF16), w_ref[0].astype(BF16))
        if mode == "silu":
            acc = _silu(acc)
        elif mode == "mul":
            acc = acc * rest[0][...].astype(F32)
        o_ref[...] = acc.astype(o_ref.dtype)

    @pl.when(b >= nb_ref[0])
    def _():
        o_ref[...] = jnp.zeros_like(o_ref)


def _gmm(x, w, tables, extra, mode, out_dtype):
    block_e, n_used = tables
    rows, k = x.shape
    n = w.shape[2]
    tb = EXPERT_BLOCK
    xmap = lambda b, be, nb: (jnp.minimum(b, nb[0] - 1), 0)
    in_specs = [
        pl.BlockSpec((tb, k), xmap),
        pl.BlockSpec((1, k, n), lambda b, be, nb: (be[b], 0, 0)),
    ]
    args = [x, w]
    if mode == "mul":
        in_specs.append(pl.BlockSpec((tb, n), xmap))
        args.append(extra)
    return pl.pallas_call(
        functools.partial(_gmm_kernel, mode=mode),
        grid_spec=pltpu.PrefetchScalarGridSpec(
            num_scalar_prefetch=2,
            grid=(rows // tb,),
            in_specs=in_specs,
            out_specs=pl.BlockSpec((tb, n), lambda b, be, nb: (b, 0)),
        ),
        out_shape=jax.ShapeDtypeStruct((rows, n), out_dtype),
        compiler_params=_cparams(("arbitrary",)),
        name="gmm_" + mode,
    )(block_e, n_used, *args)


def _swiglu_grouped(x, wg, wu, wd, tables):
    a = _gmm(x, wg, tables, None, "silu", BF16)
    h = _gmm(x, wu, tables, a, "mul", BF16)
    return _gmm(h, wd, tables, None, "plain", F32)


DSP_TT = 256


def _dispatch_kernel(zf_ref, dest_ref, u_ref, xs_ref, zero_scr, sem, zsem):
    tt = u_ref.shape[0]

    @pl.when(pl.program_id(0) == 0)
    def _():
        zero_scr[...] = jnp.zeros_like(zero_scr)

        def block_fill(b):
            rows = pl.ds(pl.multiple_of(b * EXPERT_BLOCK, EXPERT_BLOCK), EXPERT_BLOCK)
            return pltpu.make_async_copy(zero_scr, xs_ref.at[rows], zsem)

        def fill_start(b, carry):
            @pl.when(zf_ref[b] != 0)
            def _():
                block_fill(b).start()
            return carry

        def fill_wait(b, carry):
            @pl.when(zf_ref[b] != 0)
            def _():
                block_fill(b).wait()
            return carry

        lax.fori_loop(0, zf_ref.shape[0], fill_start, 0)
        lax.fori_loop(0, zf_ref.shape[0], fill_wait, 0)

    def row_copy(t, k):
        return pltpu.make_async_copy(u_ref.at[pl.ds(t, 1)], xs_ref.at[pl.ds(dest_ref[k, t], 1)], sem)

    def issue(t, carry):
        for k in range(TOP_K):
            row_copy(t, k).start()
        return carry

    def drain(t, carry):
        for k in range(TOP_K):
            row_copy(t, k).wait()
        return carry

    lax.fori_loop(0, tt, issue, 0)
    lax.fori_loop(0, tt, drain, 0)


def _dispatch(u2, dest, zero_flag, n_slots):
    t = u2.shape[0]
    tt = min(DSP_TT, t)
    return pl.pallas_call(
        _dispatch_kernel,
        grid_spec=pltpu.PrefetchScalarGridSpec(
            num_scalar_prefetch=1,
            grid=(t // tt,),
            in_specs=[pl.BlockSpec((8, tt), lambda i, zf: (0, i), memory_space=pltpu.SMEM),
                      pl.BlockSpec((tt, D_MODEL), lambda i, zf: (i, 0))],
            out_specs=pl.BlockSpec(memory_space=pl.ANY),
            scratch_shapes=[pltpu.VMEM((EXPERT_BLOCK, D_MODEL), F32),
                            pltpu.SemaphoreType.DMA(()), pltpu.SemaphoreType.DMA(())],
        ),
        out_shape=jax.ShapeDtypeStruct((n_slots, D_MODEL), F32),
        compiler_params=_cparams(("arbitrary",)),
        name="moe_dispatch",
    )(zero_flag, dest, u2)


CMB_TT = 128


def _combine_kernel(dest_ref, ys_ref, w_ref, h1_ref, sh_ref, mod_ref, ln_ref, o_ref, buf, sem):
    tt = h1_ref.shape[0]

    def row_copy(t, k):
        return pltpu.make_async_copy(ys_ref.at[pl.ds(dest_ref[k, t], 1)], buf.at[k, pl.ds(t, 1)], sem)

    def issue(t, carry):
        for k in range(TOP_K):
            row_copy(t, k).start()
        return carry

    def drain(t, carry):
        for k in range(TOP_K):
            row_copy(t, k).wait()
        return carry

    lax.fori_loop(0, tt, issue, 0)
    lax.fori_loop(0, tt, drain, 0)
    y = sh_ref[...]
    for k in range(TOP_K):
        y = y + buf[k] * w_ref[:, k:k + 1]
    gate2 = mod_ref[3:4, :]
    o_ref[...] = (_layer_norm(DEEPNORM_ALPHA * h1_ref[...] + (1.0 + gate2) * y) * ln_ref[0:1, :]
                  + ln_ref[1:2, :])


def _combine(ys, dest, w_t, h1, shared, mod3, ln2):
    t = h1.shape[0]
    tt = min(CMB_TT, t)
    row = lambda i: (i, 0)
    const = lambda i: (0, 0)
    return pl.pallas_call(
        _combine_kernel,
        grid=(t // tt,),
        in_specs=[pl.BlockSpec((8, tt), lambda i: (0, i), memory_space=pltpu.SMEM),
                  pl.BlockSpec(memory_space=pl.ANY),
                  pl.BlockSpec((tt, 8), row),
                  pl.BlockSpec((tt, D_MODEL), row),
                  pl.BlockSpec((tt, D_MODEL), row),
                  pl.BlockSpec((8, D_MODEL), const),
                  pl.BlockSpec((8, D_MODEL), const)],
        out_specs=pl.BlockSpec((tt, D_MODEL), row),
        out_shape=jax.ShapeDtypeStruct((t, D_MODEL), F32),
        scratch_shapes=[pltpu.VMEM((TOP_K, tt, D_MODEL), F32), pltpu.SemaphoreType.DMA(())],
        compiler_params=_cparams(("arbitrary",)),
        name="moe_combine",
    )(dest, ys, w_t, h1, shared, mod3, ln2)


def _pad_rows(rows, n=8):
    a = jnp.concatenate(rows, axis=0)
    return jnp.pad(a, ((0, n - a.shape[0]), (0, 0)))


def _mixer(x2, scale1, shift1, w_in, m_igate_bias, m_fgate_bias, g_conv_w, g_A_log, g_dt_bias):
    t = x2.shape[0]
    c0 = 2 * M_QK + 2 * M_V
    c1 = c0 + 4 * M_HEADS
    c2 = c1 + 4 * G_W
    w_main = jnp.concatenate([w_in[:, :c0], w_in[:, c1:c2]], axis=1).astype(BF16)
    w_gate = jnp.pad(jnp.concatenate([w_in[:, c0:c1], w_in[:, c2:]], axis=1),
                     ((0, 0), (0, GATE_PAD - N_GATE))).astype(BF16)
    proj, graw = _inproj(x2, scale1, shift1, w_main, w_gate)
    bias_row = jnp.pad(jnp.concatenate([m_igate_bias.reshape(-1), m_fgate_bias.reshape(-1),
                                        jnp.zeros((2 * G_HEADS,), F32), g_dt_bias.reshape(-1)]),
                       (0, GATE_PAD - N_GATE)).reshape(1, GATE_PAD)
    alog_row = jnp.pad(g_A_log.reshape(-1), (COL_A, GATE_PAD - N_GATE)).reshape(1, GATE_PAD)
    g_col = _gateprep(graw, bias_row, alog_row)
    g_row = jnp.swapaxes(g_col.reshape(t // CHUNK, CHUNK, GATE_PAD), 1, 2)
    mf, mb = _mlstm(proj, g_col, g_row)
    qkv = _gdn_conv(proj, jnp.pad(g_conv_w, ((0, 8 - CONV_WIDTH), (0, 0))))
    gf, gb = _gdn(qkv, g_col, g_row)
    return proj, mf, mb, gf, gb


def kernel(x, c, w_ada, b_ada, w_in, m_igate_bias, m_fgate_bias, m_norm_w, g_conv_w, g_A_log, g_dt_bias,
           g_norm_w, w_out, ln1_w, ln1_b, router_w, router_bias, e_gate, e_up, e_down, s_gate, s_up,
           s_down, ln2_w, ln2_b):
    bsz, t, d = x.shape
    assert bsz == 1 and d == D_MODEL and w_ada.shape[0] == 1
    x2 = x[0]
    mod = _ada_mod(c, w_ada[0], b_ada[0])
    shift1, scale1, gate1, shift2, scale2, gate2 = [mod[:, i * d:(i + 1) * d] for i in range(6)]
    proj, mf, mb, gf, gb = _mixer(x2, scale1, shift1, w_in[0], m_igate_bias[0], m_fgate_bias[0],
                                  g_conv_w[0], g_A_log[0], g_dt_bias[0])
    mod3 = _pad_rows([gate1, scale2, shift2, gate2])
    ln1 = _pad_rows([ln1_w[0][None], ln1_b[0][None]])
    ln2 = _pad_rows([ln2_w[0][None], ln2_b[0][None]])
    h1, u2, logits_t = _outproj(
        mf, mb, gf, gb, proj, x2, mod3, m_norm_w[0][None], jnp.tile(g_norm_w[0], G_HEADS)[None],
        w_out[0].astype(BF16), ln1, router_w[0].T)
    top_idx, top_w, rank, counts = _route(logits_t, router_bias[0][:, None])
    counts = counts[:, 0]
    n_slots = (t * TOP_K + N_EXPERTS * (EXPERT_BLOCK - 1) + EXPERT_BLOCK - 1) // EXPERT_BLOCK * EXPERT_BLOCK
    n_blocks = n_slots // EXPERT_BLOCK
    padded = (counts + EXPERT_BLOCK - 1) // EXPERT_BLOCK * EXPERT_BLOCK
    padded_end = jnp.cumsum(padded)
    group_start = padded_end - padded
    dest = jnp.take(group_start, top_idx, axis=0) + rank
    n_used = (padded_end[-1:] // EXPERT_BLOCK).astype(jnp.int32)
    blk_all = jnp.arange(n_blocks, dtype=jnp.int32) * EXPERT_BLOCK
    blk_row = jnp.minimum(blk_all, padded_end[-1] - EXPERT_BLOCK)
    block_e = jnp.sum((padded_end[None, :] <= blk_row[:, None]).astype(jnp.int32), axis=1)
    bvalid = jnp.clip(jnp.take(group_start + counts, block_e) - blk_row, 0, EXPERT_BLOCK).astype(jnp.int32)
    zero_flag = ((bvalid < EXPERT_BLOCK) | (blk_all >= padded_end[-1])).astype(jnp.int32)
    xs = _dispatch(u2, dest, zero_flag, n_slots)
    ys = _swiglu_grouped(xs, e_gate[0], e_up[0], e_down[0], (block_e, n_used))
    nsb = t // EXPERT_BLOCK
    shared_tables = (jnp.zeros((nsb,), jnp.int32), jnp.full((1,), nsb, jnp.int32))
    shared = _swiglu_grouped(u2, s_gate, s_up, s_down, shared_tables)
    out = _combine(ys, dest, top_w.T, h1, shared, mod3, ln2)
    return out[None]
```

```python
import functools

import jax
import jax.numpy as jnp
from jax import lax
from jax.experimental import pallas as pl
from jax.experimental.pallas import tpu as pltpu

F32 = jnp.float32
BF16 = jnp.bfloat16

D_MODEL = 2048
M_HEADS = 4
M_QK_DIM = 128
M_V_DIM = 256
G_HEADS = 8
G_HEAD_DIM = 128
CONV_WIDTH = 5
CHUNK = 64
GATE_SOFTCAP = 15.0
N_EXPERTS = 64
TOP_K = 6
N_GROUPS = 8
TOPK_GROUPS = 4
D_EXPERT = 1408
ROUTED_SCALE = 2.5
EXPERT_BLOCK = 256
NORM_EPS = 1e-6
DEEPNORM_ALPHA = 2.0 ** 0.25

M_QK = M_HEADS * M_QK_DIM
M_V = M_HEADS * M_V_DIM
G_W = G_HEADS * G_HEAD_DIM
N_MAIN = 2 * M_QK + 2 * M_V + 3 * G_W + G_W
N_GATE = 2 * M_HEADS + 2 * M_HEADS + 2 * G_HEADS + 2 * G_HEADS
GATE_PAD = 128
COL_I = 0
COL_F = 8
COL_B = 16
COL_A = 32
OFF_MQ, OFF_MK, OFF_MV, OFF_MO = 0, 512, 1024, 2048
OFF_GQKV, OFF_GZ = 3072, 6144

VMEM_LIMIT = 56 * 1024 * 1024


def _cparams(sem):
    return pltpu.CompilerParams(dimension_semantics=sem, vmem_limit_bytes=VMEM_LIMIT)


def _dot(a, b):
    return jnp.dot(a, b, preferred_element_type=F32)


def _dot_nt(a, b):
    return lax.dot_general(a, b, (((1,), (1,)), ((), ())), preferred_element_type=F32)


def _dot_tn(a, b):
    return lax.dot_general(a, b, (((0,), (0,)), ((), ())), preferred_element_type=F32)


def _layer_norm(x):
    mu = jnp.mean(x, axis=-1, keepdims=True)
    xc = x - mu
    var = jnp.mean(xc * xc, axis=-1, keepdims=True)
    return xc * lax.rsqrt(var + NORM_EPS)


def _sigmoid(x):
    return 1.0 / (1.0 + jnp.exp(-x))


def _silu(x):
    return x * _sigmoid(x)


def _softplus(x):
    return jnp.maximum(x, 0.0) + jnp.log1p(jnp.exp(-jnp.abs(x)))


ADA_TN = 1024
ADA_RC = 64


def _ada_kernel(c_ref, w_ref, b_ref, o_ref):
    def body(r, acc):
        rows = pl.ds(pl.multiple_of(r * ADA_RC, ADA_RC), ADA_RC)
        cond = _silu(c_ref[rows, :])
        blk = w_ref[rows, :] * cond
        return acc + jnp.sum(blk.reshape(ADA_RC // 8, 8, ADA_TN), axis=0)

    acc = lax.fori_loop(0, D_MODEL // ADA_RC, body, jnp.zeros((8, ADA_TN), F32))
    o_ref[...] = jnp.sum(acc, axis=0, keepdims=True) + b_ref[...]


def _ada_mod(c, w_ada, b_ada):
    n = w_ada.shape[1]
    return pl.pallas_call(
        _ada_kernel,
        grid=(n // ADA_TN,),
        in_specs=[
            pl.BlockSpec((D_MODEL, 1), lambda j: (0, 0)),
            pl.BlockSpec((D_MODEL, ADA_TN), lambda j: (0, j)),
            pl.BlockSpec((1, ADA_TN), lambda j: (0, j)),
        ],
        out_specs=pl.BlockSpec((1, ADA_TN), lambda j: (0, j)),
        out_shape=jax.ShapeDtypeStruct((1, n), F32),
        compiler_params=_cparams(("arbitrary",)),
        name="ada_mod",
    )(c.reshape(D_MODEL, 1), w_ada, b_ada.reshape(1, n))


INP_TM = 1024
INP_TN = 1024


def _inproj_kernel(x_ref, sc_ref, sh_ref, w_ref, wg_ref, o_ref, og_ref, u_scr):
    @pl.when(pl.program_id(1) == 0)
    def _():
        u = _layer_norm(x_ref[...]) * (1.0 + sc_ref[...]) + sh_ref[...]
        ub = u.astype(BF16)
        u_scr[...] = ub
        og_ref[...] = _dot(ub, wg_ref[...])

    o_ref[...] = _dot(u_scr[...], w_ref[...])


def _inproj(x, scale, shift, w_main, w_gate):
    t = x.shape[0]
    tm = min(INP_TM, t)
    return pl.pallas_call(
        _inproj_kernel,
        grid=(t // tm, N_MAIN // INP_TN),
        in_specs=[
            pl.BlockSpec((tm, D_MODEL), lambda i, j: (i, 0)),
            pl.BlockSpec((1, D_MODEL), lambda i, j: (0, 0)),
            pl.BlockSpec((1, D_MODEL), lambda i, j: (0, 0)),
            pl.BlockSpec((D_MODEL, INP_TN), lambda i, j: (0, j)),
            pl.BlockSpec((D_MODEL, GATE_PAD), lambda i, j: (0, 0)),
        ],
        out_specs=[
            pl.BlockSpec((tm, INP_TN), lambda i, j: (i, j)),
            pl.BlockSpec((tm, GATE_PAD), lambda i, j: (i, 0)),
        ],
        out_shape=[
            jax.ShapeDtypeStruct((t, N_MAIN), F32),
            jax.ShapeDtypeStruct((t, GATE_PAD), F32),
        ],
        scratch_shapes=[pltpu.VMEM((tm, D_MODEL), BF16)],
        compiler_params=_cparams(("arbitrary", "arbitrary")),
        name="ln_inproj",
    )(x, scale, shift, w_main, w_gate)


GP_TM = 512


def _split3(x):
    hi = x.astype(BF16)
    r1 = x - hi.astype(F32)
    mid = r1.astype(BF16)
    lo = (r1 - mid.astype(F32)).astype(BF16)
    return hi, mid, lo


def _tri_dot(tri, x):
    hi, mid, lo = _split3(x)
    return _dot(tri, hi) + _dot(tri, mid) + _dot(tri, lo)


def _gateprep_kernel(g_ref, bias_ref, alog_ref, o_ref):
    tm = g_ref.shape[0]
    lane = lax.broadcasted_iota(jnp.int32, (CHUNK, GATE_PAD), 1)
    ii = lax.broadcasted_iota(jnp.int32, (CHUNK, CHUNK), 0)
    jj = lax.broadcasted_iota(jnp.int32, (CHUNK, CHUNK), 1)
    tril = jnp.where(jj <= ii, 1.0, 0.0).astype(BF16)
    triu = jnp.where(jj >= ii, 1.0, 0.0).astype(BF16)
    neg_a = -jnp.exp(alog_ref[...])
    is_i = lane < COL_F
    is_f = (lane >= COL_F) & (lane < COL_B)
    is_b = (lane >= COL_B) & (lane < COL_A)
    fwd_cum = ((lane >= COL_F) & (lane < COL_F + M_HEADS)) | ((lane >= COL_A) & (lane < COL_A + G_HEADS))
    bwd_cum = ((lane >= COL_F + M_HEADS) & (lane < COL_B)) | ((lane >= COL_A + G_HEADS) & (lane < N_GATE))
    for c in range(tm // CHUNK):
        rows = slice(c * CHUNK, (c + 1) * CHUNK)
        x = g_ref[rows, :] + bias_ref[...]
        cap = GATE_SOFTCAP * jnp.tanh(x / GATE_SOFTCAP)
        log_f = jnp.minimum(cap, 0.0) - jnp.log1p(jnp.exp(-jnp.abs(cap)))
        beta = _sigmoid(x)
        decay = neg_a * _softplus(x)
        act = jnp.where(is_i, cap, jnp.where(is_f, log_f, jnp.where(is_b, beta, decay)))
        cum_f = _tri_dot(tril, act)
        cum_b = _tri_dot(triu, act)
        o_ref[rows, :] = jnp.where(fwd_cum, cum_f, jnp.where(bwd_cum, cum_b, act))


def _gateprep(graw, bias_row, alog_row):
    t = graw.shape[0]
    tm = min(GP_TM, t)
    return pl.pallas_call(
        _gateprep_kernel,
        grid=(t // tm,),
        in_specs=[
            pl.BlockSpec((tm, GATE_PAD), lambda i: (i, 0)),
            pl.BlockSpec((1, GATE_PAD), lambda i: (0, 0)),
            pl.BlockSpec((1, GATE_PAD), lambda i: (0, 0)),
        ],
        out_specs=pl.BlockSpec((tm, GATE_PAD), lambda i: (i, 0)),
        out_shape=jax.ShapeDtypeStruct((t, GATE_PAD), F32),
        compiler_params=_cparams(("arbitrary",)),
        name="gate_prep",
    )(graw, bias_row, alog_row)


CV_TM = 256
HALO = 8


def _conv_kernel(cur_ref, prev_ref, next_ref, cw_ref, o_ref):
    i = pl.program_id(0)
    n = pl.num_programs(0)
    tm = cur_ref.shape[0]
    pad = CONV_WIDTH // 2
    keep_prev = jnp.where(i > 0, 1.0, 0.0)
    keep_next = jnp.where(i < n - 1, 1.0, 0.0)
    for cb in range(3 * G_HEADS):
        cols = slice(cb * G_HEAD_DIM, (cb + 1) * G_HEAD_DIM)
        xp = jnp.concatenate(
            [prev_ref[:, cols] * keep_prev, cur_ref[:, cols], next_ref[:, cols] * keep_next], axis=0)
        acc = jnp.zeros((tm, G_HEAD_DIM), F32)
        for w in range(CONV_WIDTH):
            lo = HALO - pad + w
            acc = acc + xp[lo:lo + tm, :] * cw_ref[w:w + 1, cols]
        y = _silu(acc)
        if cb < 2 * G_HEADS:
            y = y * lax.rsqrt(jnp.sum(y * y, axis=-1, keepdims=True) + NORM_EPS)
            if cb < G_HEADS:
                y = y * (G_HEAD_DIM ** -0.5)
        o_ref[:, cols] = y


def _gdn_conv(proj, conv_w):
    t = proj.shape[0]
    tm = min(CV_TM, t)
    nh = tm // HALO
    nb = t // HALO
    c3 = 3 * G_W
    cblk = OFF_GQKV // c3
    return pl.pallas_call(
        _conv_kernel,
        grid=(t // tm,),
        in_specs=[
            pl.BlockSpec((tm, c3), lambda i: (i, cblk)),
            pl.BlockSpec((HALO, c3), lambda i: (jnp.maximum(i * nh - 1, 0), cblk)),
            pl.BlockSpec((HALO, c3), lambda i: (jnp.minimum((i + 1) * nh, nb - 1), cblk)),
            pl.BlockSpec((8, c3), lambda i: (0, 0)),
        ],
        out_specs=pl.BlockSpec((tm, c3), lambda i: (i, 0)),
        out_shape=jax.ShapeDtypeStruct((t, c3), F32),
        compiler_params=_cparams(("arbitrary",)),
        name="gdn_conv",
    )(proj, proj, proj, conv_w)


SCAN_RB = 512


def _tri_masks(reverse):
    ii = lax.broadcasted_iota(jnp.int32, (CHUNK, CHUNK), 0)
    jj = lax.broadcasted_iota(jnp.int32, (CHUNK, CHUNK), 1)
    if reverse:
        return jj >= ii, jj > ii
    return jj <= ii, jj < ii


def _mlstm_chunks(chains):
    for ch in chains:
        incl, _ = _tri_masks(ch["reverse"])
        ch["d"] = jnp.where(incl, ch["bc_c"] - ch["bc_r"] + ch["ic_r"], -jnp.inf)
        ch["inter"] = ch["bc_c"] + ch["m"]
        ch["qb"] = ch["q"].astype(BF16)
        ch["vb"] = ch["v"].astype(BF16)
    for ch in chains:
        ch["qk"] = _dot_nt(ch["qb"], ch["k"].astype(BF16))
    for ch in chains:
        ch["rmax"] = jnp.max(ch["d"], axis=1, keepdims=True)
    for ch in chains:
        ch["qc"] = _dot(ch["qb"], ch["c"].astype(BF16))
    for ch in chains:
        ch["qn"] = jnp.sum(ch["q"] * ch["n"], axis=1, keepdims=True)
    for ch in chains:
        bc_c = ch["bc_c"]
        b_last = bc_c[0:1, :] if ch["reverse"] else bc_c[CHUNK - 1:CHUNK, :]
        w_log = b_last - bc_c + ch["ic_c"]
        ch["m_new"] = jnp.maximum(b_last + ch["m"], jnp.max(w_log, axis=0, keepdims=True))
        ch["carry"] = jnp.exp(b_last + ch["m"] - ch["m_new"])
        ch["kw"] = ch["k"] * jnp.exp(w_log - ch["m_new"])
    for ch in chains:
        ch["kv"] = _dot_tn(ch["kw"].astype(BF16), ch["vb"])
    for ch in chains:
        ch["m_row"] = jnp.maximum(ch["rmax"], ch["inter"])
        ch["s_inter"] = jnp.exp(ch["inter"] - ch["m_row"])
        ch["w"] = jnp.exp(ch["d"] - ch["m_row"]) * ch["qk"]
    for ch in chains:
        ch["wv"] = _dot(ch["w"].astype(BF16), ch["vb"])
    for ch in chains:
        ch["wsum"] = jnp.sum(ch["w"], axis=1, keepdims=True)
    for ch in chains:
        ch["c_new"] = ch["carry"] * ch["c"] + ch["kv"]
        ch["n_new"] = ch["carry"] * ch["n"] + jnp.sum(ch["kw"], axis=0, keepdims=True)
    for ch in chains:
        num = ch["wv"] + ch["s_inter"] * ch["qc"]
        den = ch["wsum"] + ch["s_inter"] * ch["qn"]
        ch["out"] = num / jnp.maximum(jnp.abs(den), jnp.exp(-ch["m_row"]))


def _mlstm_kernel(qf_ref, kf_ref, vf_ref, gcf_ref, grf_ref,
                  qb_ref, kb_ref, vb_ref, gcb_ref, grb_ref,
                  of_ref, ob_ref, c_scr, n_scr, m_scr):
    @pl.when(pl.program_id(0) == 0)
    def _():
        c_scr[...] = jnp.zeros_like(c_scr)
        n_scr[...] = jnp.zeros_like(n_scr)
        m_scr[...] = jnp.zeros_like(m_scr)

    ncb = qf_ref.shape[0] // CHUNK
    kscale = M_QK_DIM ** -0.5

    def body(c, carry):
        chains = []
        for reverse in (False, True):
            cc = (ncb - 1 - c) if reverse else c
            rows = pl.ds(pl.multiple_of(cc * CHUNK, CHUNK), CHUNK)
            q_ref, k_ref, v_ref, gc_ref, gr_ref, o_ref = (
                (qb_ref, kb_ref, vb_ref, gcb_ref, grb_ref, ob_ref) if reverse
                else (qf_ref, kf_ref, vf_ref, gcf_ref, grf_ref, of_ref))
            gc = gc_ref[rows, :]
            gr = gr_ref[cc]
            for h in range(M_HEADS):
                s = (1 if reverse else 0) * M_HEADS + h
                ci, cf = COL_I + s, COL_F + s
                chains.append(dict(
                    reverse=reverse, s=s, o_ref=o_ref, rows=rows, h=h,
                    q=q_ref[rows, h * M_QK_DIM:(h + 1) * M_QK_DIM],
                    k=k_ref[rows, h * M_QK_DIM:(h + 1) * M_QK_DIM] * kscale,
                    v=v_ref[rows, h * M_V_DIM:(h + 1) * M_V_DIM],
                    ic_c=gc[:, ci:ci + 1], bc_c=gc[:, cf:cf + 1],
                    ic_r=gr[ci:ci + 1, :], bc_r=gr[cf:cf + 1, :],
                    c=c_scr[s], n=n_scr[s], m=m_scr[s][:, 0:1]))
        _mlstm_chunks(chains)
        for ch in chains:
            h, s = ch["h"], ch["s"]
            ch["o_ref"][ch["rows"], h * M_V_DIM:(h + 1) * M_V_DIM] = ch["out"]
            c_scr[s] = ch["c_new"]
            n_scr[s] = ch["n_new"]
            m_scr[s] = jnp.broadcast_to(ch["m_new"], (1, 128))
        return carry

    lax.fori_loop(0, ncb, body, 0)


def _mlstm(proj, g_col, g_row):
    t = proj.shape[0]
    rb = min(SCAN_RB, t)
    nb = t // rb
    ncb = rb // CHUNK
    fwd = lambda s: s
    bwd = lambda s: nb - 1 - s

    def specs(rmap):
        return [
            pl.BlockSpec((rb, M_QK), lambda s: (rmap(s), OFF_MQ // M_QK)),
            pl.BlockSpec((rb, M_QK), lambda s: (rmap(s), OFF_MK // M_QK)),
            pl.BlockSpec((rb, M_V), lambda s: (rmap(s), OFF_MV // M_V)),
            pl.BlockSpec((rb, GATE_PAD), lambda s: (rmap(s), 0)),
            pl.BlockSpec((ncb, GATE_PAD, CHUNK), lambda s: (rmap(s), 0, 0)),
        ]

    ns = 2 * M_HEADS
    return pl.pallas_call(
        _mlstm_kernel,
        grid=(nb,),
        in_specs=specs(fwd) + specs(bwd),
        out_specs=[pl.BlockSpec((rb, M_V), lambda s: (s, 0)),
                   pl.BlockSpec((rb, M_V), lambda s: (nb - 1 - s, 0))],
        out_shape=[jax.ShapeDtypeStruct((t, M_V), F32)] * 2,
        scratch_shapes=[pltpu.VMEM((ns, M_QK_DIM, M_V_DIM), F32),
                        pltpu.VMEM((ns, 1, M_QK_DIM), F32),
                        pltpu.VMEM((ns, 1, 128), F32)],
        compiler_params=_cparams(("arbitrary",)),
        name="mlstm_scan",
    )(proj, proj, proj, g_col, g_row, proj, proj, proj, g_col, g_row)


def _gdn_chunks(chains):
    ii = lax.broadcasted_iota(jnp.int32, (CHUNK, CHUNK), 0)
    jj = lax.broadcasted_iota(jnp.int32, (CHUNK, CHUNK), 1)
    eye = jnp.where(ii == jj, 1.0, 0.0)
    for ch in chains:
        incl, strict = _tri_masks(ch["reverse"])
        gc_c = ch["gc_c"]
        ch["strict"] = strict
        ch["decay"] = jnp.exp(jnp.where(incl, gc_c - ch["gc_r"], -jnp.inf))
        ch["kb16"] = ch["k"].astype(BF16)
        ch["kbeta"] = ch["k"] * ch["beta_c"]
        ch["sb"] = ch["s"].astype(BF16)
        ch["g_last"] = gc_c[0:1, :] if ch["reverse"] else gc_c[CHUNK - 1:CHUNK, :]
    for ch in chains:
        ch["kk"] = _dot_nt(ch["kbeta"].astype(BF16), ch["kb16"])
    for ch in chains:
        ch["qk"] = _dot_nt(ch["q"].astype(BF16), ch["kb16"])
    for ch in chains:
        ch["qs"] = _dot((ch["q"] * jnp.exp(ch["gc_c"])).astype(BF16), ch["sb"])
    for ch in chains:
        ch["pw"] = -jnp.where(ch["strict"], ch["kk"] * ch["decay"], 0.0)
        ch["inv"] = eye + ch["pw"]
    for _ in range(5):
        for ch in chains:
            pwb = ch["pw"].astype(BF16)
            ch["pw"] = _dot(pwb, pwb)
        for ch in chains:
            ch["inv"] = ch["inv"] + _dot(ch["inv"].astype(BF16), ch["pw"].astype(BF16))
    for ch in chains:
        rhs = jnp.concatenate([ch["v"] * ch["beta_c"], ch["kbeta"] * jnp.exp(ch["gc_c"])], axis=1)
        ch["sol"] = _dot(ch["inv"].astype(BF16), rhs.astype(BF16))
    for ch in chains:
        w = ch["sol"][:, G_HEAD_DIM:]
        ch["v_new"] = ch["sol"][:, :G_HEAD_DIM] - _dot(w.astype(BF16), ch["sb"])
        ch["vnb"] = ch["v_new"].astype(BF16)
    for ch in chains:
        kdec = ch["k"] * jnp.exp(ch["g_last"] - ch["gc_c"])
        ch["s_new"] = ch["s"] * jnp.exp(ch["g_last"]) + _dot_tn(kdec.astype(BF16), ch["vnb"])
    for ch in chains:
        attn = ch["qk"] * ch["decay"]
        ch["out"] = ch["qs"] + _dot(attn.astype(BF16), ch["vnb"])


def _gdn_kernel(qf_ref, kf_ref, vf_ref, gcf_ref, grf_ref,
                qb_ref, kb_ref, vb_ref, gcb_ref, grb_ref,
                of_ref, ob_ref, s_scr):
    @pl.when(pl.program_id(0) == 0)
    def _():
        s_scr[...] = jnp.zeros_like(s_scr)

    ncb = qf_ref.shape[0] // CHUNK

    def body(c, carry):
        chains = []
        for reverse in (False, True):
            cc = (ncb - 1 - c) if reverse else c
            rows = pl.ds(pl.multiple_of(cc * CHUNK, CHUNK), CHUNK)
            q_ref, k_ref, v_ref, gc_ref, gr_ref, o_ref = (
                (qb_ref, kb_ref, vb_ref, gcb_ref, grb_ref, ob_ref) if reverse
                else (qf_ref, kf_ref, vf_ref, gcf_ref, grf_ref, of_ref))
            gc = gc_ref[rows, :]
            gr = gr_ref[cc]
            for h in range(G_HEADS):
                s = (1 if reverse else 0) * G_HEADS + h
                cb, ca = COL_B + s, COL_A + s
                cols = slice(h * G_HEAD_DIM, (h + 1) * G_HEAD_DIM)
                chains.append(dict(
                    reverse=reverse, slot=s, o_ref=o_ref, rows=rows, cols=cols,
                    q=q_ref[rows, cols], k=k_ref[rows, cols], v=v_ref[rows, cols],
                    beta_c=gc[:, cb:cb + 1], gc_c=gc[:, ca:ca + 1], gc_r=gr[ca:ca + 1, :],
                    s=s_scr[s]))
        _gdn_chunks(chains)
        for ch in chains:
            ch["o_ref"][ch["rows"], ch["cols"]] = ch["out"]
            s_scr[ch["slot"]] = ch["s_new"]
        return carry

    lax.fori_loop(0, ncb, body, 0)


def _gdn(qkv, g_col, g_row):
    t = qkv.shape[0]
    rb = min(SCAN_RB, t)
    nb = t // rb
    ncb = rb // CHUNK
    fwd = lambda s: s
    bwd = lambda s: nb - 1 - s

    def specs(rmap):
        return [
            pl.BlockSpec((rb, G_W), lambda s: (rmap(s), 0)),
            pl.BlockSpec((rb, G_W), lambda s: (rmap(s), 1)),
            pl.BlockSpec((rb, G_W), lambda s: (rmap(s), 2)),
            pl.BlockSpec((rb, GATE_PAD), lambda s: (rmap(s), 0)),
            pl.BlockSpec((ncb, GATE_PAD, CHUNK), lambda s: (rmap(s), 0, 0)),
        ]

    return pl.pallas_call(
        _gdn_kernel,
        grid=(nb,),
        in_specs=specs(fwd) + specs(bwd),
        out_specs=[pl.BlockSpec((rb, G_W), lambda s: (s, 0)),
                   pl.BlockSpec((rb, G_W), lambda s: (nb - 1 - s, 0))],
        out_shape=[jax.ShapeDtypeStruct((t, G_W), F32)] * 2,
        scratch_shapes=[pltpu.VMEM((2 * G_HEADS, G_HEAD_DIM, G_HEAD_DIM), F32)],
        compiler_params=_cparams(("arbitrary",)),
        name="gdn_scan",
    )(qkv, qkv, qkv, g_col, g_row, qkv, qkv, qkv, g_col, g_row)


OP_TM = 256


def _outproj_kernel(mf_ref, mb_ref, gf_ref, gb_ref, mo_ref, gz_ref, x_ref, mod_ref,
                    mnw_ref, gnw_ref, wout_ref, ln_ref, rw_ref,
                    h1_ref, u2_ref, lg_ref):
    hm = mf_ref[...] + mb_ref[...]
    hg = gf_ref[...] + gb_ref[...]
    parts = []
    for h in range(M_HEADS):
        seg = hm[:, h * M_V_DIM:(h + 1) * M_V_DIM]
        parts.append(seg * lax.rsqrt(jnp.mean(seg * seg, axis=-1, keepdims=True) + NORM_EPS))
    hm_n = jnp.concatenate(parts, axis=1) * mnw_ref[...] * _sigmoid(mo_ref[...])
    parts = []
    for h in range(G_HEADS):
        seg = hg[:, h * G_HEAD_DIM:(h + 1) * G_HEAD_DIM]
        parts.append(seg * lax.rsqrt(jnp.mean(seg * seg, axis=-1, keepdims=True) + NORM_EPS))
    hg_n = jnp.concatenate(parts, axis=1) * gnw_ref[...] * _silu(gz_ref[...])
    mixed = jnp.concatenate([hm_n, hg_n], axis=1).astype(BF16)
    y = _dot(mixed, wout_ref[...])
    gate1, scale2, shift2 = mod_ref[0:1, :], mod_ref[1:2, :], mod_ref[2:3, :]
    h1 = _layer_norm(DEEPNORM_ALPHA * x_ref[...] + (1.0 + gate1) * y) * ln_ref[0:1, :] + ln_ref[1:2, :]
    h1_ref[...] = h1
    u2 = _layer_norm(h1) * (1.0 + scale2) + shift2
    u2_ref[...] = u2
    lg_ref[...] = lax.dot_general(rw_ref[...], u2, (((1,), (1,)), ((), ())),
                                  precision=lax.Precision.HIGHEST, preferred_element_type=F32)


def _outproj(mf, mb, gf, gb, proj, x, mod3, mnw, gnw, w_out, ln1, rw_t):
    t = x.shape[0]
    tm = min(OP_TM, t)
    row = lambda i: (i, 0)
    const = lambda i: (0, 0)
    return pl.pallas_call(
        _outproj_kernel,
        grid=(t // tm,),
        in_specs=[
            pl.BlockSpec((tm, M_V), row), pl.BlockSpec((tm, M_V), row),
            pl.BlockSpec((tm, G_W), row), pl.BlockSpec((tm, G_W), row),
            pl.BlockSpec((tm, M_V), lambda i: (i, OFF_MO // M_V)),
            pl.BlockSpec((tm, G_W), lambda i: (i, OFF_GZ // G_W)),
            pl.BlockSpec((tm, D_MODEL), row),
            pl.BlockSpec((8, D_MODEL), const),
            pl.BlockSpec((1, M_V), const), pl.BlockSpec((1, G_W), const),
            pl.BlockSpec((D_MODEL, D_MODEL), const),
            pl.BlockSpec((8, D_MODEL), const),
            pl.BlockSpec((N_EXPERTS, D_MODEL), const),
        ],
        out_specs=[
            pl.BlockSpec((tm, D_MODEL), row),
            pl.BlockSpec((tm, D_MODEL), row),
            pl.BlockSpec((N_EXPERTS, tm), lambda i: (0, i)),
        ],
        out_shape=[
            jax.ShapeDtypeStruct((t, D_MODEL), F32),
            jax.ShapeDtypeStruct((t, D_MODEL), F32),
            jax.ShapeDtypeStruct((N_EXPERTS, t), F32),
        ],
        compiler_params=_cparams(("arbitrary",)),
        name="outproj_ln",
    )(mf, mb, gf, gb, proj, proj, x, mod3, mnw, gnw, w_out, ln1, rw_t)


RT_TN = 512
GROUP_SIZE = N_EXPERTS // N_GROUPS


def _route_kernel(lg_ref, bias_ref, idx_ref, w_ref, rank_ref, cnt_ref, carry_scr):
    tn = lg_ref.shape[1]

    @pl.when(pl.program_id(0) == 0)
    def _():
        carry_scr[...] = jnp.zeros_like(carry_scr)

    neg = -jnp.inf
    scores = _sigmoid(lg_ref[...])
    biased = scores + bias_ref[...]
    sub8 = lax.broadcasted_iota(jnp.int32, (GROUP_SIZE, tn), 0).astype(F32)
    gscore = []
    for g in range(N_GROUPS):
        bg = biased[g * GROUP_SIZE:(g + 1) * GROUP_SIZE, :]
        m1 = jnp.max(bg, axis=0, keepdims=True)
        first = jnp.min(jnp.where(bg == m1, sub8, float(GROUP_SIZE)), axis=0, keepdims=True)
        m2 = jnp.max(jnp.where(sub8 == first, neg, bg), axis=0, keepdims=True)
        gscore.append(m1 + m2)
    masked = []
    for g in range(N_GROUPS):
        beaten = jnp.zeros((1, tn), F32)
        for g2 in range(N_GROUPS):
            if g2 == g:
                continue
            wins = (gscore[g2] >= gscore[g]) if g2 < g else (gscore[g2] > gscore[g])
            beaten = beaten + jnp.where(wins, 1.0, 0.0)
        keep = beaten < float(TOPK_GROUPS)
        masked.append(jnp.where(keep, biased[g * GROUP_SIZE:(g + 1) * GROUP_SIZE, :], neg))
    x = jnp.concatenate(masked, axis=0)
    eidx = lax.broadcasted_iota(jnp.int32, (N_EXPERTS, tn), 0).astype(F32)
    member = jnp.zeros((N_EXPERTS, tn), F32)
    sel_idx, sel_s = [], []
    for _ in range(TOP_K):
        m = jnp.max(x, axis=0, keepdims=True)
        idx = jnp.min(jnp.where(x == m, eidx, float(N_EXPERTS)), axis=0, keepdims=True)
        sel = eidx == idx
        sel_idx.append(idx)
        sel_s.append(jnp.sum(jnp.where(sel, scores, 0.0), axis=0, keepdims=True))
        member = member + jnp.where(sel, 1.0, 0.0)
        x = jnp.where(sel, neg, x)
    total = sel_s[0]
    for s in sel_s[1:]:
        total = total + s
    ti = lax.broadcasted_iota(jnp.int32, (tn, tn), 0)
    tj = lax.broadcasted_iota(jnp.int32, (tn, tn), 1)
    before = jnp.where(ti < tj, 1.0, 0.0).astype(BF16)
    carry = carry_scr[:, 0:1]
    excl = _dot(member.astype(BF16), before) + carry
    ranks = [jnp.sum(jnp.where(eidx == i, excl, 0.0), axis=0, keepdims=True) for i in sel_idx]
    zero = jnp.zeros((1, tn), F32)
    idx_ref[...] = jnp.concatenate(sel_idx + [zero, zero], axis=0).astype(jnp.int32)
    w_ref[...] = jnp.concatenate([ROUTED_SCALE * s / total for s in sel_s] + [zero, zero], axis=0)
    rank_ref[...] = jnp.concatenate(ranks + [zero, zero], axis=0).astype(jnp.int32)
    new_carry = carry + jnp.sum(member, axis=1, keepdims=True)
    carry_scr[...] = jnp.broadcast_to(new_carry, carry_scr.shape)
    cnt_ref[...] = jnp.broadcast_to(new_carry, cnt_ref.shape).astype(jnp.int32)


def _route(logits_t, bias_col):
    t = logits_t.shape[1]
    tn = min(RT_TN, t)
    col = lambda i: (0, i)
    return pl.pallas_call(
        _route_kernel,
        grid=(t // tn,),
        in_specs=[pl.BlockSpec((N_EXPERTS, tn), col), pl.BlockSpec((N_EXPERTS, 1), lambda i: (0, 0))],
        out_specs=[pl.BlockSpec((8, tn), col), pl.BlockSpec((8, tn), col), pl.BlockSpec((8, tn), col),
                   pl.BlockSpec((N_EXPERTS, 128), lambda i: (0, 0))],
        out_shape=[jax.ShapeDtypeStruct((8, t), jnp.int32), jax.ShapeDtypeStruct((8, t), F32),
                   jax.ShapeDtypeStruct((8, t), jnp.int32), jax.ShapeDtypeStruct((N_EXPERTS, 128), jnp.int32)],
        scratch_shapes=[pltpu.VMEM((N_EXPERTS, 128), F32)],
        compiler_params=_cparams(("arbitrary",)),
        name="route_topk",
    )(logits_t, bias_col)


def _gmm_kernel(be_ref, nb_ref, first_ref, slot_ref, nxt_ref, x_ref, w_hbm, *rest, mode):
    wbuf, sem = rest[-2], rest[-1]
    o_ref = rest[-3]
    b = pl.program_id(0)

    def weight_copy(e, s):
        return pltpu.make_async_copy(w_hbm.at[e], wbuf.at[s], sem.at[s])

    @pl.when(b == 0)
    def _():
        weight_copy(be_ref[0], 0).start()

    @pl.when(first_ref[b] == 1)
    def _():
        weight_copy(be_ref[b], slot_ref[b]).wait()

        @pl.when(nxt_ref[b] >= 0)
        def _():
            weight_copy(nxt_ref[b], 1 - slot_ref[b]).start()

    @pl.when(b < nb_ref[0])
    def _():
        acc = _dot(x_ref[...].astype(BF16), wbuf[slot_ref[b]].astype(BF16))
        if mode == "silu":
            acc = _silu(acc)
        elif mode == "mul":
            acc = acc * rest[0][...].astype(F32)
        o_ref[...] = acc.astype(o_ref.dtype)

    @pl.when(b >= nb_ref[0])
    def _():
        o_ref[...] = jnp.zeros_like(o_ref)


def _run_tables(block_e, n_used):
    nb = block_e.shape[0]
    idx = jnp.arange(nb, dtype=jnp.int32)
    used = idx < n_used[0]
    prev = jnp.concatenate([jnp.full((1,), -1, jnp.int32), block_e[:-1]])
    first = ((block_e != prev) & used).astype(jnp.int32)
    slot = (jnp.cumsum(first) - 1) & 1
    later_first = jnp.where(first == 1, idx, nb)
    nxt_pos = jnp.flip(lax.cummin(jnp.flip(jnp.concatenate([later_first[1:], jnp.full((1,), nb, jnp.int32)]))))
    nxt = jnp.where(nxt_pos < nb, jnp.take(block_e, jnp.minimum(nxt_pos, nb - 1)), -1)
    return first, slot.astype(jnp.int32), nxt.astype(jnp.int32)


def _gmm(x, w, tables, extra, mode, out_dtype):
    block_e, n_used, first, slot, nxt = tables
    rows, k = x.shape
    n = w.shape[2]
    tb = EXPERT_BLOCK
    xmap = lambda b, be, nb, fi, sl, nx: (jnp.minimum(b, nb[0] - 1), 0)
    in_specs = [pl.BlockSpec((tb, k), xmap), pl.BlockSpec(memory_space=pl.ANY)]
    args = [x, w]
    if mode == "mul":
        in_specs.append(pl.BlockSpec((tb, n), xmap))
        args.append(extra)
    return pl.pallas_call(
        functools.partial(_gmm_kernel, mode=mode),
        grid_spec=pltpu.PrefetchScalarGridSpec(
            num_scalar_prefetch=5,
            grid=(rows // tb,),
            in_specs=in_specs,
            out_specs=pl.BlockSpec((tb, n), lambda b, be, nb, fi, sl, nx: (b, 0)),
            scratch_shapes=[pltpu.VMEM((2, k, n), F32), pltpu.SemaphoreType.DMA((2,))],
        ),
        out_shape=jax.ShapeDtypeStruct((rows, n), out_dtype),
        compiler_params=_cparams(("arbitrary",)),
        name="gmm_" + mode,
    )(block_e, n_used, first, slot, nxt, *args)


def _swiglu_grouped(x, wg, wu, wd, tables):
    a = _gmm(x, wg, tables, None, "silu", BF16)
    h = _gmm(x, wu, tables, a, "mul", BF16)
    return _gmm(h, wd, tables, None, "plain", F32)


DSP_TT = 256


def _dispatch_kernel(zf_ref, dest_ref, u_ref, xs_ref, zero_scr, sem, zsem):
    tt = u_ref.shape[0]

    @pl.when(pl.program_id(0) == 0)
    def _():
        zero_scr[...] = jnp.zeros_like(zero_scr)

        def block_fill(b):
            rows = pl.ds(pl.multiple_of(b * EXPERT_BLOCK, EXPERT_BLOCK), EXPERT_BLOCK)
            return pltpu.make_async_copy(zero_scr, xs_ref.at[rows], zsem)

        def fill_start(b, carry):
            @pl.when(zf_ref[b] != 0)
            def _():
                block_fill(b).start()
            return carry

        def fill_wait(b, carry):
            @pl.when(zf_ref[b] != 0)
            def _():
                block_fill(b).wait()
            return carry

        lax.fori_loop(0, zf_ref.shape[0], fill_start, 0)
        lax.fori_loop(0, zf_ref.shape[0], fill_wait, 0)

    def row_copy(t, k):
        return pltpu.make_async_copy(u_ref.at[pl.ds(t, 1)], xs_ref.at[pl.ds(dest_ref[k, t], 1)], sem)

    def issue(t, carry):
        for k in range(TOP_K):
            row_copy(t, k).start(priority=k % 2)
        return carry

    def drain(t, carry):
        for k in range(TOP_K):
            row_copy(t, k).wait()
        return carry

    lax.fori_loop(0, tt, issue, 0)
    lax.fori_loop(0, tt, drain, 0)


def _dispatch(u2, dest, zero_flag, n_slots):
    t = u2.shape[0]
    tt = min(DSP_TT, t)
    return pl.pallas_call(
        _dispatch_kernel,
        grid_spec=pltpu.PrefetchScalarGridSpec(
            num_scalar_prefetch=1,
            grid=(t // tt,),
            in_specs=[pl.BlockSpec((8, tt), lambda i, zf: (0, i), memory_space=pltpu.SMEM),
                      pl.BlockSpec((tt, D_MODEL), lambda i, zf: (i, 0))],
            out_specs=pl.BlockSpec(memory_space=pl.ANY),
            scratch_shapes=[pltpu.VMEM((EXPERT_BLOCK, D_MODEL), F32),
                            pltpu.SemaphoreType.DMA(()), pltpu.SemaphoreType.DMA(())],
        ),
        out_shape=jax.ShapeDtypeStruct((n_slots, D_MODEL), F32),
        compiler_params=_cparams(("arbitrary",)),
        name="moe_dispatch",
    )(zero_flag, dest, u2)


CMB_TT = 128


def _combine_kernel(dest_ref, ys_ref, w_ref, h1_ref, sh_ref, mod_ref, ln_ref, o_ref, buf, sem):
    tt = h1_ref.shape[0]

    def row_copy(t, k):
        return pltpu.make_async_copy(ys_ref.at[pl.ds(dest_ref[k, t], 1)], buf.at[k, pl.ds(t, 1)], sem)

    def issue(t, carry):
        for k in range(TOP_K):
            row_copy(t, k).start(priority=k % 2)
        return carry

    def drain(t, carry):
        for k in range(TOP_K):
            row_copy(t, k).wait()
        return carry

    lax.fori_loop(0, tt, issue, 0)
    lax.fori_loop(0, tt, drain, 0)
    y = sh_ref[...]
    for k in range(TOP_K):
        y = y + buf[k] * w_ref[:, k:k + 1]
    gate2 = mod_ref[3:4, :]
    o_ref[...] = (_layer_norm(DEEPNORM_ALPHA * h1_ref[...] + (1.0 + gate2) * y) * ln_ref[0:1, :]
                  + ln_ref[1:2, :])


def _combine(ys, dest, w_t, h1, shared, mod3, ln2):
    t = h1.shape[0]
    tt = min(CMB_TT, t)
    row = lambda i: (i, 0)
    const = lambda i: (0, 0)
    return pl.pallas_call(
        _combine_kernel,
        grid=(t // tt,),
        in_specs=[pl.BlockSpec((8, tt), lambda i: (0, i), memory_space=pltpu.SMEM),
                  pl.BlockSpec(memory_space=pl.ANY),
                  pl.BlockSpec((tt, 8), row),
                  pl.BlockSpec((tt, D_MODEL), row),
                  pl.BlockSpec((tt, D_MODEL), row),
                  pl.BlockSpec((8, D_MODEL), const),
                  pl.BlockSpec((8, D_MODEL), const)],
        out_specs=pl.BlockSpec((tt, D_MODEL), row),
        out_shape=jax.ShapeDtypeStruct((t, D_MODEL), F32),
        scratch_shapes=[pltpu.VMEM((TOP_K, tt, D_MODEL), F32), pltpu.SemaphoreType.DMA(())],
        compiler_params=_cparams(("arbitrary",)),
        name="moe_combine",
    )(dest, ys, w_t, h1, shared, mod3, ln2)


def _pad_rows(rows, n=8):
    a = jnp.concatenate(rows, axis=0)
    return jnp.pad(a, ((0, n - a.shape[0]), (0, 0)))


def _mixer(x2, scale1, shift1, w_in, m_igate_bias, m_fgate_bias, g_conv_w, g_A_log, g_dt_bias):
    t = x2.shape[0]
    c0 = 2 * M_QK + 2 * M_V
    c1 = c0 + 4 * M_HEADS
    c2 = c1 + 4 * G_W
    w_main = jnp.concatenate([w_in[:, :c0], w_in[:, c1:c2]], axis=1).astype(BF16)
    w_gate = jnp.pad(jnp.concatenate([w_in[:, c0:c1], w_in[:, c2:]], axis=1),
                     ((0, 0), (0, GATE_PAD - N_GATE))).astype(BF16)
    proj, graw = _inproj(x2, scale1, shift1, w_main, w_gate)
    bias_row = jnp.pad(jnp.concatenate([m_igate_bias.reshape(-1), m_fgate_bias.reshape(-1),
                                        jnp.zeros((2 * G_HEADS,), F32), g_dt_bias.reshape(-1)]),
                       (0, GATE_PAD - N_GATE)).reshape(1, GATE_PAD)
    alog_row = jnp.pad(g_A_log.reshape(-1), (COL_A, GATE_PAD - N_GATE)).reshape(1, GATE_PAD)
    g_col = _gateprep(graw, bias_row, alog_row)
    g_row = jnp.swapaxes(g_col.reshape(t // CHUNK, CHUNK, GATE_PAD), 1, 2)
    mf, mb = _mlstm(proj, g_col, g_row)
    qkv = _gdn_conv(proj, jnp.pad(g_conv_w, ((0, 8 - CONV_WIDTH), (0, 0))))
    gf, gb = _gdn(qkv, g_col, g_row)
    return proj, mf, mb, gf, gb


def kernel(x, c, w_ada, b_ada, w_in, m_igate_bias, m_fgate_bias, m_norm_w, g_conv_w, g_A_log, g_dt_bias,
           g_norm_w, w_out, ln1_w, ln1_b, router_w, router_bias, e_gate, e_up, e_down, s_gate, s_up,
           s_down, ln2_w, ln2_b):
    bsz, t, d = x.shape
    assert bsz == 1 and d == D_MODEL and w_ada.shape[0] == 1
    x2 = x[0]
    mod = _ada_mod(c, w_ada[0], b_ada[0])
    shift1, scale1, gate1, shift2, scale2, gate2 = [mod[:, i * d:(i + 1) * d] for i in range(6)]
    proj, mf, mb, gf, gb = _mixer(x2, scale1, shift1, w_in[0], m_igate_bias[0], m_fgate_bias[0],
                                  g_conv_w[0], g_A_log[0], g_dt_bias[0])
    mod3 = _pad_rows([gate1, scale2, shift2, gate2])
    ln1 = _pad_rows([ln1_w[0][None], ln1_b[0][None]])
    ln2 = _pad_rows([ln2_w[0][None], ln2_b[0][None]])
    h1, u2, logits_t = _outproj(
        mf, mb, gf, gb, proj, x2, mod3, m_norm_w[0][None], jnp.tile(g_norm_w[0], G_HEADS)[None],
        w_out[0].astype(BF16), ln1, router_w[0].T)
    top_idx, top_w, rank, counts = _route(logits_t, router_bias[0][:, None])
    counts = counts[:, 0]
    n_slots = (t * TOP_K + N_EXPERTS * (EXPERT_BLOCK - 1) + EXPERT_BLOCK - 1) // EXPERT_BLOCK * EXPERT_BLOCK
    n_blocks = n_slots // EXPERT_BLOCK
    padded = (counts + EXPERT_BLOCK - 1) // EXPERT_BLOCK * EXPERT_BLOCK
    padded_end = jnp.cumsum(padded)
    group_start = padded_end - padded
    expert_ids = jnp.arange(N_EXPERTS, dtype=jnp.int32)[:, None, None]
    dest = jnp.sum(jnp.where(top_idx[None] == expert_ids, group_start[:, None, None], 0), axis=0) + rank
    n_used = (padded_end[-1:] // EXPERT_BLOCK).astype(jnp.int32)
    blk_all = jnp.arange(n_blocks, dtype=jnp.int32) * EXPERT_BLOCK
    blk_row = jnp.minimum(blk_all, padded_end[-1] - EXPERT_BLOCK)
    block_e = jnp.sum((padded_end[None, :] <= blk_row[:, None]).astype(jnp.int32), axis=1)
    bvalid = jnp.clip(jnp.take(group_start + counts, block_e) - blk_row, 0, EXPERT_BLOCK).astype(jnp.int32)
    zero_flag = ((bvalid < EXPERT_BLOCK) | (blk_all >= padded_end[-1])).astype(jnp.int32)
    xs = _dispatch(u2, dest, zero_flag, n_slots)
    ys = _swiglu_grouped(xs, e_gate[0], e_up[0], e_down[0],
                         (block_e, n_used) + _run_tables(block_e, n_used))
    nsb = t // EXPERT_BLOCK
    shared_be, shared_nb = jnp.zeros((nsb,), jnp.int32), jnp.full((1,), nsb, jnp.int32)
    shared = _swiglu_grouped(u2, s_gate, s_up, s_down,
                             (shared_be, shared_nb) + _run_tables(shared_be, shared_nb))
    out = _combine(ys, dest, top_w.T, h1, shared, mod3, ln2)
    return out[None]
```

```python
import functools

import jax
import jax.numpy as jnp
from jax import lax
from jax.experimental import pallas as pl
from jax.experimental.pallas import tpu as pltpu

F32 = jnp.float32
BF16 = jnp.bfloat16

D_MODEL = 2048
M_HEADS = 4
M_QK_DIM = 128
M_V_DIM = 256
G_HEADS = 8
G_HEAD_DIM = 128
CONV_WIDTH = 5
CHUNK = 64
GATE_SOFTCAP = 15.0
N_EXPERTS = 64
TOP_K = 6
N_GROUPS = 8
TOPK_GROUPS = 4
D_EXPERT = 1408
ROUTED_SCALE = 2.5
EXPERT_BLOCK = 256
NORM_EPS = 1e-6
DEEPNORM_ALPHA = 2.0 ** 0.25

M_QK = M_HEADS * M_QK_DIM
M_V = M_HEADS * M_V_DIM
G_W = G_HEADS * G_HEAD_DIM
N_MAIN = 2 * M_QK + 2 * M_V + 3 * G_W + G_W
N_GATE = 2 * M_HEADS + 2 * M_HEADS + 2 * G_HEADS + 2 * G_HEADS
GATE_PAD = 128
COL_I = 0
COL_F = 8
COL_B = 16
COL_A = 32
OFF_MQ, OFF_MK, OFF_MV, OFF_MO = 0, 512, 1024, 2048
OFF_GQKV, OFF_GZ = 3072, 6144

VMEM_LIMIT = 56 * 1024 * 1024


def _cparams(sem):
    return pltpu.CompilerParams(dimension_semantics=sem, vmem_limit_bytes=VMEM_LIMIT)


def _dot(a, b):
    return jnp.dot(a, b, preferred_element_type=F32)


def _dot_nt(a, b):
    return lax.dot_general(a, b, (((1,), (1,)), ((), ())), preferred_element_type=F32)


def _dot_tn(a, b):
    return lax.dot_general(a, b, (((0,), (0,)), ((), ())), preferred_element_type=F32)


def _layer_norm(x):
    mu = jnp.mean(x, axis=-1, keepdims=True)
    xc = x - mu
    var = jnp.mean(xc * xc, axis=-1, keepdims=True)
    return xc * lax.rsqrt(var + NORM_EPS)


def _sigmoid(x):
    return 1.0 / (1.0 + jnp.exp(-x))


def _silu(x):
    return x * _sigmoid(x)


def _softplus(x):
    return jnp.maximum(x, 0.0) + jnp.log1p(jnp.exp(-jnp.abs(x)))


ADA_TN = 1024
ADA_RC = 64


def _ada_kernel(c_ref, w_ref, b_ref, o_ref):
    def body(r, acc):
        rows = pl.ds(pl.multiple_of(r * ADA_RC, ADA_RC), ADA_RC)
        cond = _silu(c_ref[rows, :])
        blk = w_ref[rows, :] * cond
        return acc + jnp.sum(blk.reshape(ADA_RC // 8, 8, ADA_TN), axis=0)

    acc = lax.fori_loop(0, D_MODEL // ADA_RC, body, jnp.zeros((8, ADA_TN), F32))
    o_ref[...] = jnp.sum(acc, axis=0, keepdims=True) + b_ref[...]


def _ada_mod(c, w_ada, b_ada):
    n = w_ada.shape[1]
    return pl.pallas_call(
        _ada_kernel,
        grid=(n // ADA_TN,),
        in_specs=[
            pl.BlockSpec((D_MODEL, 1), lambda j: (0, 0)),
            pl.BlockSpec((D_MODEL, ADA_TN), lambda j: (0, j)),
            pl.BlockSpec((1, ADA_TN), lambda j: (0, j)),
        ],
        out_specs=pl.BlockSpec((1, ADA_TN), lambda j: (0, j)),
        out_shape=jax.ShapeDtypeStruct((1, n), F32),
        compiler_params=_cparams(("arbitrary",)),
        name="ada_mod",
    )(c.reshape(D_MODEL, 1), w_ada, b_ada.reshape(1, n))


INP_TM = 1024
INP_TN = 1024


def _inproj_kernel(x_ref, sc_ref, sh_ref, w_ref, wg_ref, o_ref, og_ref, u_scr):
    @pl.when(pl.program_id(1) == 0)
    def _():
        u = _layer_norm(x_ref[...]) * (1.0 + sc_ref[...]) + sh_ref[...]
        ub = u.astype(BF16)
        u_scr[...] = ub
        og_ref[...] = _dot(ub, wg_ref[...])

    o_ref[...] = _dot(u_scr[...], w_ref[...])


def _inproj(x, scale, shift, w_main, w_gate):
    t = x.shape[0]
    tm = min(INP_TM, t)
    return pl.pallas_call(
        _inproj_kernel,
        grid=(t // tm, N_MAIN // INP_TN),
        in_specs=[
            pl.BlockSpec((tm, D_MODEL), lambda i, j: (i, 0)),
            pl.BlockSpec((1, D_MODEL), lambda i, j: (0, 0)),
            pl.BlockSpec((1, D_MODEL), lambda i, j: (0, 0)),
            pl.BlockSpec((D_MODEL, INP_TN), lambda i, j: (0, j)),
            pl.BlockSpec((D_MODEL, GATE_PAD), lambda i, j: (0, 0)),
        ],
        out_specs=[
            pl.BlockSpec((tm, INP_TN), lambda i, j: (i, j)),
            pl.BlockSpec((tm, GATE_PAD), lambda i, j: (i, 0)),
        ],
        out_shape=[
            jax.ShapeDtypeStruct((t, N_MAIN), F32),
            jax.ShapeDtypeStruct((t, GATE_PAD), F32),
        ],
        scratch_shapes=[pltpu.VMEM((tm, D_MODEL), BF16)],
        compiler_params=_cparams(("arbitrary", "arbitrary")),
        name="ln_inproj",
    )(x, scale, shift, w_main, w_gate)


GP_TM = 512


def _split3(x):
    hi = x.astype(BF16)
    r1 = x - hi.astype(F32)
    mid = r1.astype(BF16)
    lo = (r1 - mid.astype(F32)).astype(BF16)
    return hi, mid, lo


def _tri_dot(tri, x):
    hi, mid, lo = _split3(x)
    return _dot(tri, hi) + _dot(tri, mid) + _dot(tri, lo)


def _gateprep_kernel(g_ref, bias_ref, alog_ref, o_ref):
    tm = g_ref.shape[0]
    lane = lax.broadcasted_iota(jnp.int32, (CHUNK, GATE_PAD), 1)
    ii = lax.broadcasted_iota(jnp.int32, (CHUNK, CHUNK), 0)
    jj = lax.broadcasted_iota(jnp.int32, (CHUNK, CHUNK), 1)
    tril = jnp.where(jj <= ii, 1.0, 0.0).astype(BF16)
    triu = jnp.where(jj >= ii, 1.0, 0.0).astype(BF16)
    neg_a = -jnp.exp(alog_ref[...])
    is_i = lane < COL_F
    is_f = (lane >= COL_F) & (lane < COL_B)
    is_b = (lane >= COL_B) & (lane < COL_A)
    fwd_cum = ((lane >= COL_F) & (lane < COL_F + M_HEADS)) | ((lane >= COL_A) & (lane < COL_A + G_HEADS))
    bwd_cum = ((lane >= COL_F + M_HEADS) & (lane < COL_B)) | ((lane >= COL_A + G_HEADS) & (lane < N_GATE))
    for c in range(tm // CHUNK):
        rows = slice(c * CHUNK, (c + 1) * CHUNK)
        x = g_ref[rows, :] + bias_ref[...]
        cap = GATE_SOFTCAP * jnp.tanh(x / GATE_SOFTCAP)
        log_f = jnp.minimum(cap, 0.0) - jnp.log1p(jnp.exp(-jnp.abs(cap)))
        beta = _sigmoid(x)
        decay = neg_a * _softplus(x)
        act = jnp.where(is_i, cap, jnp.where(is_f, log_f, jnp.where(is_b, beta, decay)))
        cum_f = _tri_dot(tril, act)
        cum_b = _tri_dot(triu, act)
        o_ref[rows, :] = jnp.where(fwd_cum, cum_f, jnp.where(bwd_cum, cum_b, act))


def _gateprep(graw, bias_row, alog_row):
    t = graw.shape[0]
    tm = min(GP_TM, t)
    return pl.pallas_call(
        _gateprep_kernel,
        grid=(t // tm,),
        in_specs=[
            pl.BlockSpec((tm, GATE_PAD), lambda i: (i, 0)),
            pl.BlockSpec((1, GATE_PAD), lambda i: (0, 0)),
            pl.BlockSpec((1, GATE_PAD), lambda i: (0, 0)),
        ],
        out_specs=pl.BlockSpec((tm, GATE_PAD), lambda i: (i, 0)),
        out_shape=jax.ShapeDtypeStruct((t, GATE_PAD), F32),
        compiler_params=_cparams(("arbitrary",)),
        name="gate_prep",
    )(graw, bias_row, alog_row)


CV_TM = 256
HALO = 8


def _conv_kernel(cur_ref, prev_ref, next_ref, cw_ref, o_ref):
    i = pl.program_id(0)
    n = pl.num_programs(0)
    tm = cur_ref.shape[0]
    pad = CONV_WIDTH // 2
    keep_prev = jnp.where(i > 0, 1.0, 0.0)
    keep_next = jnp.where(i < n - 1, 1.0, 0.0)
    for cb in range(3 * G_HEADS):
        cols = slice(cb * G_HEAD_DIM, (cb + 1) * G_HEAD_DIM)
        xp = jnp.concatenate(
            [prev_ref[:, cols] * keep_prev, cur_ref[:, cols], next_ref[:, cols] * keep_next], axis=0)
        acc = jnp.zeros((tm, G_HEAD_DIM), F32)
        for w in range(CONV_WIDTH):
            lo = HALO - pad + w
            acc = acc + xp[lo:lo + tm, :] * cw_ref[w:w + 1, cols]
        y = _silu(acc)
        if cb < 2 * G_HEADS:
            y = y * lax.rsqrt(jnp.sum(y * y, axis=-1, keepdims=True) + NORM_EPS)
            if cb < G_HEADS:
                y = y * (G_HEAD_DIM ** -0.5)
        o_ref[:, cols] = y


def _gdn_conv(proj, conv_w):
    t = proj.shape[0]
    tm = min(CV_TM, t)
    nh = tm // HALO
    nb = t // HALO
    c3 = 3 * G_W
    cblk = OFF_GQKV // c3
    return pl.pallas_call(
        _conv_kernel,
        grid=(t // tm,),
        in_specs=[
            pl.BlockSpec((tm, c3), lambda i: (i, cblk)),
            pl.BlockSpec((HALO, c3), lambda i: (jnp.maximum(i * nh - 1, 0), cblk)),
            pl.BlockSpec((HALO, c3), lambda i: (jnp.minimum((i + 1) * nh, nb - 1), cblk)),
            pl.BlockSpec((8, c3), lambda i: (0, 0)),
        ],
        out_specs=pl.BlockSpec((tm, c3), lambda i: (i, 0)),
        out_shape=jax.ShapeDtypeStruct((t, c3), F32),
        compiler_params=_cparams(("arbitrary",)),
        name="gdn_conv",
    )(proj, proj, proj, conv_w)


SCAN_RB = 512


def _tri_masks(reverse):
    ii = lax.broadcasted_iota(jnp.int32, (CHUNK, CHUNK), 0)
    jj = lax.broadcasted_iota(jnp.int32, (CHUNK, CHUNK), 1)
    if reverse:
        return jj >= ii, jj > ii
    return jj <= ii, jj < ii


def _mlstm_chunks(chains):
    for ch in chains:
        incl, _ = _tri_masks(ch["reverse"])
        ch["d"] = jnp.where(incl, ch["bc_c"] - ch["bc_r"] + ch["ic_r"], -jnp.inf)
        ch["inter"] = ch["bc_c"] + ch["m"]
        ch["qb"] = ch["q"].astype(BF16)
        ch["vb"] = ch["v"].astype(BF16)
    for ch in chains:
        ch["qk"] = _dot_nt(ch["qb"], ch["k"].astype(BF16))
    for ch in chains:
        ch["rmax"] = jnp.max(ch["d"], axis=1, keepdims=True)
    for ch in chains:
        ch["qc"] = _dot(ch["qb"], ch["c"].astype(BF16))
    for ch in chains:
        ch["qn"] = jnp.sum(ch["q"] * ch["n"], axis=1, keepdims=True)
    for ch in chains:
        bc_c = ch["bc_c"]
        b_last = bc_c[0:1, :] if ch["reverse"] else bc_c[CHUNK - 1:CHUNK, :]
        w_log = b_last - bc_c + ch["ic_c"]
        ch["m_new"] = jnp.maximum(b_last + ch["m"], jnp.max(w_log, axis=0, keepdims=True))
        ch["carry"] = jnp.exp(b_last + ch["m"] - ch["m_new"])
        ch["kw"] = ch["k"] * jnp.exp(w_log - ch["m_new"])
    for ch in chains:
        ch["kv"] = _dot_tn(ch["kw"].astype(BF16), ch["vb"])
    for ch in chains:
        ch["m_row"] = jnp.maximum(ch["rmax"], ch["inter"])
        ch["s_inter"] = jnp.exp(ch["inter"] - ch["m_row"])
        ch["w"] = jnp.exp(ch["d"] - ch["m_row"]) * ch["qk"]
    for ch in chains:
        ch["wv"] = _dot(ch["w"].astype(BF16), ch["vb"])
    for ch in chains:
        ch["wsum"] = jnp.sum(ch["w"], axis=1, keepdims=True)
    for ch in chains:
        ch["c_new"] = ch["carry"] * ch["c"] + ch["kv"]
        ch["n_new"] = ch["carry"] * ch["n"] + jnp.sum(ch["kw"], axis=0, keepdims=True)
    for ch in chains:
        num = ch["wv"] + ch["s_inter"] * ch["qc"]
        den = ch["wsum"] + ch["s_inter"] * ch["qn"]
        ch["out"] = num / jnp.maximum(jnp.abs(den), jnp.exp(-ch["m_row"]))


def _mlstm_kernel(qf_ref, kf_ref, vf_ref, gcf_ref, grf_ref,
                  qb_ref, kb_ref, vb_ref, gcb_ref, grb_ref,
                  of_ref, ob_ref, c_scr, n_scr, m_scr):
    @pl.when(pl.program_id(0) == 0)
    def _():
        c_scr[...] = jnp.zeros_like(c_scr)
        n_scr[...] = jnp.zeros_like(n_scr)
        m_scr[...] = jnp.zeros_like(m_scr)

    ncb = qf_ref.shape[0] // CHUNK
    kscale = M_QK_DIM ** -0.5

    def body(c, carry):
        chains = []
        for reverse in (False, True):
            cc = (ncb - 1 - c) if reverse else c
            rows = pl.ds(pl.multiple_of(cc * CHUNK, CHUNK), CHUNK)
            q_ref, k_ref, v_ref, gc_ref, gr_ref, o_ref = (
                (qb_ref, kb_ref, vb_ref, gcb_ref, grb_ref, ob_ref) if reverse
                else (qf_ref, kf_ref, vf_ref, gcf_ref, grf_ref, of_ref))
            gc = gc_ref[rows, :]
            gr = gr_ref[cc]
            for h in range(M_HEADS):
                s = (1 if reverse else 0) * M_HEADS + h
                ci, cf = COL_I + s, COL_F + s
                chains.append(dict(
                    reverse=reverse, s=s, o_ref=o_ref, rows=rows, h=h,
                    q=q_ref[rows, h * M_QK_DIM:(h + 1) * M_QK_DIM],
                    k=k_ref[rows, h * M_QK_DIM:(h + 1) * M_QK_DIM] * kscale,
                    v=v_ref[rows, h * M_V_DIM:(h + 1) * M_V_DIM],
                    ic_c=gc[:, ci:ci + 1], bc_c=gc[:, cf:cf + 1],
                    ic_r=gr[ci:ci + 1, :], bc_r=gr[cf:cf + 1, :],
                    c=c_scr[s], n=n_scr[s], m=m_scr[s][:, 0:1]))
        _mlstm_chunks(chains)
        for ch in chains:
            h, s = ch["h"], ch["s"]
            ch["o_ref"][ch["rows"], h * M_V_DIM:(h + 1) * M_V_DIM] = ch["out"]
            c_scr[s] = ch["c_new"]
            n_scr[s] = ch["n_new"]
            m_scr[s] = jnp.broadcast_to(ch["m_new"], (1, 128))
        return carry

    lax.fori_loop(0, ncb, body, 0)


def _mlstm(proj, g_col, g_row):
    t = proj.shape[0]
    rb = min(SCAN_RB, t)
    nb = t // rb
    ncb = rb // CHUNK
    fwd = lambda s: s
    bwd = lambda s: nb - 1 - s

    def specs(rmap):
        return [
            pl.BlockSpec((rb, M_QK), lambda s: (rmap(s), OFF_MQ // M_QK)),
            pl.BlockSpec((rb, M_QK), lambda s: (rmap(s), OFF_MK // M_QK)),
            pl.BlockSpec((rb, M_V), lambda s: (rmap(s), OFF_MV // M_V)),
            pl.BlockSpec((rb, GATE_PAD), lambda s: (rmap(s), 0)),
            pl.BlockSpec((ncb, GATE_PAD, CHUNK), lambda s: (rmap(s), 0, 0)),
        ]

    ns = 2 * M_HEADS
    return pl.pallas_call(
        _mlstm_kernel,
        grid=(nb,),
        in_specs=specs(fwd) + specs(bwd),
        out_specs=[pl.BlockSpec((rb, M_V), lambda s: (s, 0)),
                   pl.BlockSpec((rb, M_V), lambda s: (nb - 1 - s, 0))],
        out_shape=[jax.ShapeDtypeStruct((t, M_V), F32)] * 2,
        scratch_shapes=[pltpu.VMEM((ns, M_QK_DIM, M_V_DIM), F32),
                        pltpu.VMEM((ns, 1, M_QK_DIM), F32),
                        pltpu.VMEM((ns, 1, 128), F32)],
        compiler_params=_cparams(("arbitrary",)),
        name="mlstm_scan",
    )(proj, proj, proj, g_col, g_row, proj, proj, proj, g_col, g_row)


def _gdn_chunks(chains):
    ii = lax.broadcasted_iota(jnp.int32, (CHUNK, CHUNK), 0)
    jj = lax.broadcasted_iota(jnp.int32, (CHUNK, CHUNK), 1)
    eye = jnp.where(ii == jj, 1.0, 0.0)
    for ch in chains:
        incl, strict = _tri_masks(ch["reverse"])
        gc_c = ch["gc_c"]
        ch["strict"] = strict
        ch["decay"] = jnp.exp(jnp.where(incl, gc_c - ch["gc_r"], -jnp.inf))
        ch["kb16"] = ch["k"].astype(BF16)
        ch["kbeta"] = ch["k"] * ch["beta_c"]
        ch["sb"] = ch["s"].astype(BF16)
        ch["g_last"] = gc_c[0:1, :] if ch["reverse"] else gc_c[CHUNK - 1:CHUNK, :]
    for ch in chains:
        ch["kk"] = _dot_nt(ch["kbeta"].astype(BF16), ch["kb16"])
    for ch in chains:
        ch["qk"] = _dot_nt(ch["q"].astype(BF16), ch["kb16"])
    for ch in chains:
        ch["qs"] = _dot((ch["q"] * jnp.exp(ch["gc_c"])).astype(BF16), ch["sb"])
    for ch in chains:
        ch["pw"] = -jnp.where(ch["strict"], ch["kk"] * ch["decay"], 0.0)
        ch["inv"] = eye + ch["pw"]
    for _ in range(5):
        for ch in chains:
            pwb = ch["pw"].astype(BF16)
            ch["pw"] = _dot(pwb, pwb)
        for ch in chains:
            ch["inv"] = ch["inv"] + _dot(ch["inv"].astype(BF16), ch["pw"].astype(BF16))
    for ch in chains:
        rhs = jnp.concatenate([ch["v"] * ch["beta_c"], ch["kbeta"] * jnp.exp(ch["gc_c"])], axis=1)
        ch["sol"] = _dot(ch["inv"].astype(BF16), rhs.astype(BF16))
    for ch in chains:
        w = ch["sol"][:, G_HEAD_DIM:]
        ch["v_new"] = ch["sol"][:, :G_HEAD_DIM] - _dot(w.astype(BF16), ch["sb"])
        ch["vnb"] = ch["v_new"].astype(BF16)
    for ch in chains:
        kdec = ch["k"] * jnp.exp(ch["g_last"] - ch["gc_c"])
        ch["s_new"] = ch["s"] * jnp.exp(ch["g_last"]) + _dot_tn(kdec.astype(BF16), ch["vnb"])
    for ch in chains:
        attn = ch["qk"] * ch["decay"]
        ch["out"] = ch["qs"] + _dot(attn.astype(BF16), ch["vnb"])


def _gdn_kernel(qf_ref, kf_ref, vf_ref, gcf_ref, grf_ref,
                qb_ref, kb_ref, vb_ref, gcb_ref, grb_ref,
                of_ref, ob_ref, s_scr):
    @pl.when(pl.program_id(0) == 0)
    def _():
        s_scr[...] = jnp.zeros_like(s_scr)

    ncb = qf_ref.shape[0] // CHUNK

    def body(c, carry):
        chains = []
        for reverse in (False, True):
            cc = (ncb - 1 - c) if reverse else c
            rows = pl.ds(pl.multiple_of(cc * CHUNK, CHUNK), CHUNK)
            q_ref, k_ref, v_ref, gc_ref, gr_ref, o_ref = (
                (qb_ref, kb_ref, vb_ref, gcb_ref, grb_ref, ob_ref) if reverse
                else (qf_ref, kf_ref, vf_ref, gcf_ref, grf_ref, of_ref))
            gc = gc_ref[rows, :]
            gr = gr_ref[cc]
            for h in range(G_HEADS):
                s = (1 if reverse else 0) * G_HEADS + h
                cb, ca = COL_B + s, COL_A + s
                cols = slice(h * G_HEAD_DIM, (h + 1) * G_HEAD_DIM)
                chains.append(dict(
                    reverse=reverse, slot=s, o_ref=o_ref, rows=rows, cols=cols,
                    q=q_ref[rows, cols], k=k_ref[rows, cols], v=v_ref[rows, cols],
                    beta_c=gc[:, cb:cb + 1], gc_c=gc[:, ca:ca + 1], gc_r=gr[ca:ca + 1, :],
                    s=s_scr[s]))
        _gdn_chunks(chains)
        for ch in chains:
            ch["o_ref"][ch["rows"], ch["cols"]] = ch["out"]
            s_scr[ch["slot"]] = ch["s_new"]
        return carry

    lax.fori_loop(0, ncb, body, 0)


def _gdn(qkv, g_col, g_row):
    t = qkv.shape[0]
    rb = min(SCAN_RB, t)
    nb = t // rb
    ncb = rb // CHUNK
    fwd = lambda s: s
    bwd = lambda s: nb - 1 - s

    def specs(rmap):
        return [
            pl.BlockSpec((rb, G_W), lambda s: (rmap(s), 0)),
            pl.BlockSpec((rb, G_W), lambda s: (rmap(s), 1)),
            pl.BlockSpec((rb, G_W), lambda s: (rmap(s), 2)),
            pl.BlockSpec((rb, GATE_PAD), lambda s: (rmap(s), 0)),
            pl.BlockSpec((ncb, GATE_PAD, CHUNK), lambda s: (rmap(s), 0, 0)),
        ]

    return pl.pallas_call(
        _gdn_kernel,
        grid=(nb,),
        in_specs=specs(fwd) + specs(bwd),
        out_specs=[pl.BlockSpec((rb, G_W), lambda s: (s, 0)),
                   pl.BlockSpec((rb, G_W), lambda s: (nb - 1 - s, 0))],
        out_shape=[jax.ShapeDtypeStruct((t, G_W), F32)] * 2,
        scratch_shapes=[pltpu.VMEM((2 * G_HEADS, G_HEAD_DIM, G_HEAD_DIM), F32)],
        compiler_params=_cparams(("arbitrary",)),
        name="gdn_scan",
    )(qkv, qkv, qkv, g_col, g_row, qkv, qkv, qkv, g_col, g_row)


OP_TM = 256
OP_SUB = 128


def _outproj_kernel(mf_ref, mb_ref, gf_ref, gb_ref, mo_ref, gz_ref, x_ref, mod_ref,
                    mnw_ref, gnw_ref, wout_ref, ln_ref, rw_ref,
                    h1_ref, u2_ref, lg_ref):
    gate1, scale2, shift2 = mod_ref[0:1, :], mod_ref[1:2, :], mod_ref[2:3, :]
    subs = [slice(r, r + OP_SUB) for r in range(0, x_ref.shape[0], OP_SUB)]
    mixed, y, h1s, u2s = {}, {}, {}, {}
    for i, rows in enumerate(subs):
        hm = mf_ref[rows, :] + mb_ref[rows, :]
        hg = gf_ref[rows, :] + gb_ref[rows, :]
        parts = []
        for h in range(M_HEADS):
            seg = hm[:, h * M_V_DIM:(h + 1) * M_V_DIM]
            parts.append(seg * lax.rsqrt(jnp.mean(seg * seg, axis=-1, keepdims=True) + NORM_EPS))
        hm_n = jnp.concatenate(parts, axis=1) * mnw_ref[...] * _sigmoid(mo_ref[rows, :])
        parts = []
        for h in range(G_HEADS):
            seg = hg[:, h * G_HEAD_DIM:(h + 1) * G_HEAD_DIM]
            parts.append(seg * lax.rsqrt(jnp.mean(seg * seg, axis=-1, keepdims=True) + NORM_EPS))
        hg_n = jnp.concatenate(parts, axis=1) * gnw_ref[...] * _silu(gz_ref[rows, :])
        mixed[i] = jnp.concatenate([hm_n, hg_n], axis=1).astype(BF16)
    for i, rows in enumerate(subs):
        y[i] = _dot(mixed[i], wout_ref[...])
    for i, rows in enumerate(subs):
        h1 = (_layer_norm(DEEPNORM_ALPHA * x_ref[rows, :] + (1.0 + gate1) * y[i]) * ln_ref[0:1, :]
              + ln_ref[1:2, :])
        h1_ref[rows, :] = h1
        h1s[i] = h1
    for i, rows in enumerate(subs):
        u2 = _layer_norm(h1s[i]) * (1.0 + scale2) + shift2
        u2_ref[rows, :] = u2
        u2s[i] = u2
    for i, rows in enumerate(subs):
        lg_ref[:, rows] = lax.dot_general(rw_ref[...], u2s[i], (((1,), (1,)), ((), ())),
                                          precision=lax.Precision.HIGHEST, preferred_element_type=F32)


def _outproj(mf, mb, gf, gb, proj, x, mod3, mnw, gnw, w_out, ln1, rw_t):
    t = x.shape[0]
    tm = min(OP_TM, t)
    row = lambda i: (i, 0)
    const = lambda i: (0, 0)
    return pl.pallas_call(
        _outproj_kernel,
        grid=(t // tm,),
        in_specs=[
            pl.BlockSpec((tm, M_V), row), pl.BlockSpec((tm, M_V), row),
            pl.BlockSpec((tm, G_W), row), pl.BlockSpec((tm, G_W), row),
            pl.BlockSpec((tm, M_V), lambda i: (i, OFF_MO // M_V)),
            pl.BlockSpec((tm, G_W), lambda i: (i, OFF_GZ // G_W)),
            pl.BlockSpec((tm, D_MODEL), row),
            pl.BlockSpec((8, D_MODEL), const),
            pl.BlockSpec((1, M_V), const), pl.BlockSpec((1, G_W), const),
            pl.BlockSpec((D_MODEL, D_MODEL), const),
            pl.BlockSpec((8, D_MODEL), const),
            pl.BlockSpec((N_EXPERTS, D_MODEL), const),
        ],
        out_specs=[
            pl.BlockSpec((tm, D_MODEL), row),
            pl.BlockSpec((tm, D_MODEL), row),
            pl.BlockSpec((N_EXPERTS, tm), lambda i: (0, i)),
        ],
        out_shape=[
            jax.ShapeDtypeStruct((t, D_MODEL), F32),
            jax.ShapeDtypeStruct((t, D_MODEL), F32),
            jax.ShapeDtypeStruct((N_EXPERTS, t), F32),
        ],
        compiler_params=_cparams(("arbitrary",)),
        name="outproj_ln",
    )(mf, mb, gf, gb, proj, proj, x, mod3, mnw, gnw, w_out, ln1, rw_t)


RT_TN = 512
GROUP_SIZE = N_EXPERTS // N_GROUPS


def _route_kernel(lg_ref, bias_ref, idx_ref, w_ref, rank_ref, cnt_ref, carry_scr):
    tn = lg_ref.shape[1]

    @pl.when(pl.program_id(0) == 0)
    def _():
        carry_scr[...] = jnp.zeros_like(carry_scr)

    neg = -jnp.inf
    scores = _sigmoid(lg_ref[...])
    biased = scores + bias_ref[...]
    sub8 = lax.broadcasted_iota(jnp.int32, (GROUP_SIZE, tn), 0).astype(F32)
    gscore = []
    for g in range(N_GROUPS):
        bg = biased[g * GROUP_SIZE:(g + 1) * GROUP_SIZE, :]
        m1 = jnp.max(bg, axis=0, keepdims=True)
        first = jnp.min(jnp.where(bg == m1, sub8, float(GROUP_SIZE)), axis=0, keepdims=True)
        m2 = jnp.max(jnp.where(sub8 == first, neg, bg), axis=0, keepdims=True)
        gscore.append(m1 + m2)
    masked = []
    for g in range(N_GROUPS):
        beaten = jnp.zeros((1, tn), F32)
        for g2 in range(N_GROUPS):
            if g2 == g:
                continue
            wins = (gscore[g2] >= gscore[g]) if g2 < g else (gscore[g2] > gscore[g])
            beaten = beaten + jnp.where(wins, 1.0, 0.0)
        keep = beaten < float(TOPK_GROUPS)
        masked.append(jnp.where(keep, biased[g * GROUP_SIZE:(g + 1) * GROUP_SIZE, :], neg))
    x = jnp.concatenate(masked, axis=0)
    eidx = lax.broadcasted_iota(jnp.int32, (N_EXPERTS, tn), 0).astype(F32)
    member = jnp.zeros((N_EXPERTS, tn), F32)
    sel_idx, sel_s = [], []
    for _ in range(TOP_K):
        m = jnp.max(x, axis=0, keepdims=True)
        idx = jnp.min(jnp.where(x == m, eidx, float(N_EXPERTS)), axis=0, keepdims=True)
        sel = eidx == idx
        sel_idx.append(idx)
        sel_s.append(jnp.sum(jnp.where(sel, scores, 0.0), axis=0, keepdims=True))
        member = member + jnp.where(sel, 1.0, 0.0)
        x = jnp.where(sel, neg, x)
    total = sel_s[0]
    for s in sel_s[1:]:
        total = total + s
    ti = lax.broadcasted_iota(jnp.int32, (tn, tn), 0)
    tj = lax.broadcasted_iota(jnp.int32, (tn, tn), 1)
    before = jnp.where(ti < tj, 1.0, 0.0).astype(BF16)
    carry = carry_scr[:, 0:1]
    excl = _dot(member.astype(BF16), before) + carry
    ranks = [jnp.sum(jnp.where(eidx == i, excl, 0.0), axis=0, keepdims=True) for i in sel_idx]
    zero = jnp.zeros((1, tn), F32)
    idx_ref[...] = jnp.concatenate(sel_idx + [zero, zero], axis=0).astype(jnp.int32)
    w_ref[...] = jnp.concatenate([ROUTED_SCALE * s / total for s in sel_s] + [zero, zero], axis=0)
    rank_ref[...] = jnp.concatenate(ranks + [zero, zero], axis=0).astype(jnp.int32)
    new_carry = carry + jnp.sum(member, axis=1, keepdims=True)
    carry_scr[...] = jnp.broadcast_to(new_carry, carry_scr.shape)
    cnt_ref[...] = jnp.broadcast_to(new_carry, cnt_ref.shape).astype(jnp.int32)


def _route(logits_t, bias_col):
    t = logits_t.shape[1]
    tn = min(RT_TN, t)
    col = lambda i: (0, i)
    return pl.pallas_call(
        _route_kernel,
        grid=(t // tn,),
        in_specs=[pl.BlockSpec((N_EXPERTS, tn), col), pl.BlockSpec((N_EXPERTS, 1), lambda i: (0, 0))],
        out_specs=[pl.BlockSpec((8, tn), col), pl.BlockSpec((8, tn), col), pl.BlockSpec((8, tn), col),
                   pl.BlockSpec((N_EXPERTS, 128), lambda i: (0, 0))],
        out_shape=[jax.ShapeDtypeStruct((8, t), jnp.int32), jax.ShapeDtypeStruct((8, t), F32),
                   jax.ShapeDtypeStruct((8, t), jnp.int32), jax.ShapeDtypeStruct((N_EXPERTS, 128), jnp.int32)],
        scratch_shapes=[pltpu.VMEM((N_EXPERTS, 128), F32)],
        compiler_params=_cparams(("arbitrary",)),
        name="route_topk",
    )(logits_t, bias_col)


def _gmm_kernel(be_ref, nb_ref, first_ref, slot_ref, nxt_ref, x_ref, w_hbm, *rest, mode):
    wbuf, sem = rest[-2], rest[-1]
    o_ref = rest[-3]
    tb = EXPERT_BLOCK
    b0 = pl.program_id(0) * 2
    b1 = b0 + 1
    nb = nb_ref[0]

    def weight_copy(e, s):
        return pltpu.make_async_copy(w_hbm.at[e], wbuf.at[s], sem.at[s])

    def enter_run(b):
        @pl.when(first_ref[b] == 1)
        def _():
            weight_copy(be_ref[b], slot_ref[b]).wait()

            @pl.when(nxt_ref[b] >= 0)
            def _():
                weight_copy(nxt_ref[b], 1 - slot_ref[b]).start()

    def compute(rows, b):
        acc = _dot(x_ref[rows, :].astype(BF16), wbuf[slot_ref[b]].astype(BF16))
        if mode == "silu":
            acc = _silu(acc)
        elif mode == "mul":
            acc = acc * rest[0][rows, :].astype(F32)
        o_ref[rows, :] = acc.astype(o_ref.dtype)

    def block(rows, b):
        @pl.when(b < nb)
        def _():
            compute(rows, b)

        @pl.when(b >= nb)
        def _():
            o_ref[rows, :] = jnp.zeros((tb, o_ref.shape[1]), o_ref.dtype)

    @pl.when(b0 == 0)
    def _():
        weight_copy(be_ref[0], 0).start()

    enter_run(b0)
    same_run = (b1 < nb) & (first_ref[b1] == 0)

    @pl.when(same_run)
    def _():
        compute(slice(0, 2 * tb), b0)

    @pl.when(jnp.logical_not(same_run))
    def _():
        block(slice(0, tb), b0)
        enter_run(b1)
        block(slice(tb, 2 * tb), b1)


def _run_tables(block_e, n_used):
    nb = block_e.shape[0]
    idx = jnp.arange(nb, dtype=jnp.int32)
    used = idx < n_used[0]
    prev = jnp.concatenate([jnp.full((1,), -1, jnp.int32), block_e[:-1]])
    first = ((block_e != prev) & used).astype(jnp.int32)
    slot = (jnp.cumsum(first) - 1) & 1
    later_first = jnp.where(first == 1, idx, nb)
    nxt_pos = jnp.flip(lax.cummin(jnp.flip(jnp.concatenate([later_first[1:], jnp.full((1,), nb, jnp.int32)]))))
    nxt = jnp.where(nxt_pos < nb, jnp.take(block_e, jnp.minimum(nxt_pos, nb - 1)), -1)
    return first, slot.astype(jnp.int32), nxt.astype(jnp.int32)


def _gmm(x, w, tables, extra, mode, out_dtype):
    block_e, n_used, first, slot, nxt = tables
    rows, k = x.shape
    n = w.shape[2]
    tb = 2 * EXPERT_BLOCK
    xmap = lambda p, be, nb, fi, sl, nx: (jnp.minimum(p, (nb[0] - 1) // 2), 0)
    in_specs = [pl.BlockSpec((tb, k), xmap), pl.BlockSpec(memory_space=pl.ANY)]
    args = [x, w]
    if mode == "mul":
        in_specs.append(pl.BlockSpec((tb, n), xmap))
        args.append(extra)
    return pl.pallas_call(
        functools.partial(_gmm_kernel, mode=mode),
        grid_spec=pltpu.PrefetchScalarGridSpec(
            num_scalar_prefetch=5,
            grid=(rows // tb,),
            in_specs=in_specs,
            out_specs=pl.BlockSpec((tb, n), lambda b, be, nb, fi, sl, nx: (b, 0)),
            scratch_shapes=[pltpu.VMEM((2, k, n), F32), pltpu.SemaphoreType.DMA((2,))],
        ),
        out_shape=jax.ShapeDtypeStruct((rows, n), out_dtype),
        compiler_params=_cparams(("arbitrary",)),
        name="gmm_" + mode,
    )(block_e, n_used, first, slot, nxt, *args)


def _swiglu_grouped(x, wg, wu, wd, tables):
    a = _gmm(x, wg, tables, None, "silu", BF16)
    h = _gmm(x, wu, tables, a, "mul", BF16)
    return _gmm(h, wd, tables, None, "plain", F32)


DSP_TT = 256
ROW_UNROLL = 8


def _dispatch_kernel(zf_ref, dest_ref, u_ref, xs_ref, zero_scr, sem, zsem):
    tt = u_ref.shape[0]

    @pl.when(pl.program_id(0) == 0)
    def _():
        zero_scr[...] = jnp.zeros_like(zero_scr)

        def block_fill(b):
            rows = pl.ds(pl.multiple_of(b * EXPERT_BLOCK, EXPERT_BLOCK), EXPERT_BLOCK)
            return pltpu.make_async_copy(zero_scr, xs_ref.at[rows], zsem)

        def fill_start(b, carry):
            @pl.when(zf_ref[b] != 0)
            def _():
                block_fill(b).start()
            return carry

        def fill_wait(b, carry):
            @pl.when(zf_ref[b] != 0)
            def _():
                block_fill(b).wait()
            return carry

        lax.fori_loop(0, zf_ref.shape[0], fill_start, 0)
        lax.fori_loop(0, zf_ref.shape[0], fill_wait, 0)

    def row_copy(t, k):
        return pltpu.make_async_copy(u_ref.at[pl.ds(t, 1)], xs_ref.at[pl.ds(dest_ref[k, t], 1)], sem)

    _row_dma_loops(tt, row_copy)


def _row_dma_loops(tt, row_copy):
    def issue(g, carry):
        t0 = pl.multiple_of(g * ROW_UNROLL, ROW_UNROLL)
        for r in range(ROW_UNROLL):
            for k in range(TOP_K):
                row_copy(t0 + r, k).start(priority=k % 2)
        return carry

    def drain(g, carry):
        t0 = pl.multiple_of(g * ROW_UNROLL, ROW_UNROLL)
        for r in range(ROW_UNROLL):
            for k in range(TOP_K):
                row_copy(t0 + r, k).wait()
        return carry

    lax.fori_loop(0, tt // ROW_UNROLL, issue, 0)
    lax.fori_loop(0, tt // ROW_UNROLL, drain, 0)


def _dispatch(u2, dest, zero_flag, n_slots):
    t = u2.shape[0]
    tt = min(DSP_TT, t)
    return pl.pallas_call(
        _dispatch_kernel,
        grid_spec=pltpu.PrefetchScalarGridSpec(
            num_scalar_prefetch=1,
            grid=(t // tt,),
            in_specs=[pl.BlockSpec((8, tt), lambda i, zf: (0, i), memory_space=pltpu.SMEM),
                      pl.BlockSpec((tt, D_MODEL), lambda i, zf: (i, 0))],
            out_specs=pl.BlockSpec(memory_space=pl.ANY),
            scratch_shapes=[pltpu.VMEM((EXPERT_BLOCK, D_MODEL), F32),
                            pltpu.SemaphoreType.DMA(()), pltpu.SemaphoreType.DMA(())],
        ),
        out_shape=jax.ShapeDtypeStruct((n_slots, D_MODEL), F32),
        compiler_params=_cparams(("arbitrary",)),
        name="moe_dispatch",
    )(zero_flag, dest, u2)


CMB_TT = 128


def _combine_kernel(dest_ref, ys_ref, w_ref, h1_ref, sh_ref, mod_ref, ln_ref, o_ref, buf, sem):
    tt = h1_ref.shape[0]

    def row_copy(t, k):
        return pltpu.make_async_copy(ys_ref.at[pl.ds(dest_ref[k, t], 1)], buf.at[k, pl.ds(t, 1)], sem)

    _row_dma_loops(tt, row_copy)
    y = sh_ref[...]
    for k in range(TOP_K):
        y = y + buf[k] * w_ref[:, k:k + 1]
    gate2 = mod_ref[3:4, :]
    o_ref[...] = (_layer_norm(DEEPNORM_ALPHA * h1_ref[...] + (1.0 + gate2) * y) * ln_ref[0:1, :]
                  + ln_ref[1:2, :])


def _combine(ys, dest, w_t, h1, shared, mod3, ln2):
    t = h1.shape[0]
    tt = min(CMB_TT, t)
    row = lambda i: (i, 0)
    const = lambda i: (0, 0)
    return pl.pallas_call(
        _combine_kernel,
        grid=(t // tt,),
        in_specs=[pl.BlockSpec((8, tt), lambda i: (0, i), memory_space=pltpu.SMEM),
                  pl.BlockSpec(memory_space=pl.ANY),
                  pl.BlockSpec((tt, 8), row),
                  pl.BlockSpec((tt, D_MODEL), row),
                  pl.BlockSpec((tt, D_MODEL), row),
                  pl.BlockSpec((8, D_MODEL), const),
                  pl.BlockSpec((8, D_MODEL), const)],
        out_specs=pl.BlockSpec((tt, D_MODEL), row),
        out_shape=jax.ShapeDtypeStruct((t, D_MODEL), F32),
        scratch_shapes=[pltpu.VMEM((TOP_K, tt, D_MODEL), F32), pltpu.SemaphoreType.DMA(())],
        compiler_params=_cparams(("arbitrary",)),
        name="moe_combine",
    )(dest, ys, w_t, h1, shared, mod3, ln2)


def _pad_rows(rows, n=8):
    a = jnp.concatenate(rows, axis=0)
    return jnp.pad(a, ((0, n - a.shape[0]), (0, 0)))


def _mixer(x2, scale1, shift1, w_in, m_igate_bias, m_fgate_bias, g_conv_w, g_A_log, g_dt_bias):
    t = x2.shape[0]
    c0 = 2 * M_QK + 2 * M_V
    c1 = c0 + 4 * M_HEADS
    c2 = c1 + 4 * G_W
    w_main = jnp.concatenate([w_in[:, :c0], w_in[:, c1:c2]], axis=1).astype(BF16)
    w_gate = jnp.pad(jnp.concatenate([w_in[:, c0:c1], w_in[:, c2:]], axis=1),
                     ((0, 0), (0, GATE_PAD - N_GATE))).astype(BF16)
    proj, graw = _inproj(x2, scale1, shift1, w_main, w_gate)
    bias_row = jnp.pad(jnp.concatenate([m_igate_bias.reshape(-1), m_fgate_bias.reshape(-1),
                                        jnp.zeros((2 * G_HEADS,), F32), g_dt_bias.reshape(-1)]),
                       (0, GATE_PAD - N_GATE)).reshape(1, GATE_PAD)
    alog_row = jnp.pad(g_A_log.reshape(-1), (COL_A, GATE_PAD - N_GATE)).reshape(1, GATE_PAD)
    g_col = _gateprep(graw, bias_row, alog_row)
    g_row = jnp.swapaxes(g_col.reshape(t // CHUNK, CHUNK, GATE_PAD), 1, 2)
    mf, mb = _mlstm(proj, g_col, g_row)
    qkv = _gdn_conv(proj, jnp.pad(g_conv_w, ((0, 8 - CONV_WIDTH), (0, 0))))
    gf, gb = _gdn(qkv, g_col, g_row)
    return proj, mf, mb, gf, gb


def kernel(x, c, w_ada, b_ada, w_in, m_igate_bias, m_fgate_bias, m_norm_w, g_conv_w, g_A_log, g_dt_bias,
           g_norm_w, w_out, ln1_w, ln1_b, router_w, router_bias, e_gate, e_up, e_down, s_gate, s_up,
           s_down, ln2_w, ln2_b):
    bsz, t, d = x.shape
    assert bsz == 1 and d == D_MODEL and w_ada.shape[0] == 1
    x2 = x[0]
    mod = _ada_mod(c, w_ada[0], b_ada[0])
    shift1, scale1, gate1, shift2, scale2, gate2 = [mod[:, i * d:(i + 1) * d] for i in range(6)]
    proj, mf, mb, gf, gb = _mixer(x2, scale1, shift1, w_in[0], m_igate_bias[0], m_fgate_bias[0],
                                  g_conv_w[0], g_A_log[0], g_dt_bias[0])
    mod3 = _pad_rows([gate1, scale2, shift2, gate2])
    ln1 = _pad_rows([ln1_w[0][None], ln1_b[0][None]])
    ln2 = _pad_rows([ln2_w[0][None], ln2_b[0][None]])
    h1, u2, logits_t = _outproj(
        mf, mb, gf, gb, proj, x2, mod3, m_norm_w[0][None], jnp.tile(g_norm_w[0], G_HEADS)[None],
        w_out[0].astype(BF16), ln1, router_w[0].T)
    top_idx, top_w, rank, counts = _route(logits_t, router_bias[0][:, None])
    counts = counts[:, 0]
    n_slots = (t * TOP_K + N_EXPERTS * (EXPERT_BLOCK - 1) + EXPERT_BLOCK - 1) // EXPERT_BLOCK * EXPERT_BLOCK
    n_blocks = n_slots // EXPERT_BLOCK
    padded = (counts + EXPERT_BLOCK - 1) // EXPERT_BLOCK * EXPERT_BLOCK
    padded_end = jnp.cumsum(padded)
    group_start = padded_end - padded
    expert_ids = jnp.arange(N_EXPERTS, dtype=jnp.int32)[:, None, None]
    dest = jnp.sum(jnp.where(top_idx[None] == expert_ids, group_start[:, None, None], 0), axis=0) + rank
    n_used = (padded_end[-1:] // EXPERT_BLOCK).astype(jnp.int32)
    blk_all = jnp.arange(n_blocks, dtype=jnp.int32) * EXPERT_BLOCK
    blk_row = jnp.minimum(blk_all, padded_end[-1] - EXPERT_BLOCK)
    block_e = jnp.sum((padded_end[None, :] <= blk_row[:, None]).astype(jnp.int32), axis=1)
    bvalid = jnp.clip(jnp.take(group_start + counts, block_e) - blk_row, 0, EXPERT_BLOCK).astype(jnp.int32)
    zero_flag = ((bvalid < EXPERT_BLOCK) | (blk_all >= padded_end[-1])).astype(jnp.int32)
    xs = _dispatch(u2, dest, zero_flag, n_slots)
    ys = _swiglu_grouped(xs, e_gate[0], e_up[0], e_down[0],
                         (block_e, n_used) + _run_tables(block_e, n_used))
    nsb = t // EXPERT_BLOCK
    shared_be, shared_nb = jnp.zeros((nsb,), jnp.int32), jnp.full((1,), nsb, jnp.int32)
    shared = _swiglu_grouped(u2, s_gate, s_up, s_down,
                             (shared_be, shared_nb) + _run_tables(shared_be, shared_nb))
    out = _combine(ys, dest, top_w.T, h1, shared, mod3, ln2)
    return out[None]
```

```python
import functools

import jax
import jax.numpy as jnp
from jax import lax
from jax.experimental import pallas as pl
from jax.experimental.pallas import tpu as pltpu

F32 = jnp.float32
BF16 = jnp.bfloat16

D_MODEL = 2048
M_HEADS = 4
M_QK_DIM = 128
M_V_DIM = 256
G_HEADS = 8
G_HEAD_DIM = 128
CONV_WIDTH = 5
CHUNK = 64
GATE_SOFTCAP = 15.0
N_EXPERTS = 64
TOP_K = 6
N_GROUPS = 8
TOPK_GROUPS = 4
D_EXPERT = 1408
ROUTED_SCALE = 2.5
EXPERT_BLOCK = 256
NORM_EPS = 1e-6
DEEPNORM_ALPHA = 2.0 ** 0.25

M_QK = M_HEADS * M_QK_DIM
M_V = M_HEADS * M_V_DIM
G_W = G_HEADS * G_HEAD_DIM
N_MAIN = 2 * M_QK + 2 * M_V + 3 * G_W + G_W
N_GATE = 2 * M_HEADS + 2 * M_HEADS + 2 * G_HEADS + 2 * G_HEADS
GATE_PAD = 128
COL_I = 0
COL_F = 8
COL_B = 16
COL_A = 32
OFF_MQ, OFF_MK, OFF_MV, OFF_MO = 0, 512, 1024, 2048
OFF_GQKV, OFF_GZ = 3072, 6144

VMEM_LIMIT = 56 * 1024 * 1024


def _cparams(sem):
    return pltpu.CompilerParams(dimension_semantics=sem, vmem_limit_bytes=VMEM_LIMIT)


def _dot(a, b):
    return jnp.dot(a, b, preferred_element_type=F32)


def _dot_nt(a, b):
    return lax.dot_general(a, b, (((1,), (1,)), ((), ())), preferred_element_type=F32)


def _dot_tn(a, b):
    return lax.dot_general(a, b, (((0,), (0,)), ((), ())), preferred_element_type=F32)


def _layer_norm(x):
    mu = jnp.mean(x, axis=-1, keepdims=True)
    xc = x - mu
    var = jnp.mean(xc * xc, axis=-1, keepdims=True)
    return xc * lax.rsqrt(var + NORM_EPS)


def _sigmoid(x):
    return 1.0 / (1.0 + jnp.exp(-x))


def _silu(x):
    return x * _sigmoid(x)


def _softplus(x):
    return jnp.maximum(x, 0.0) + jnp.log1p(jnp.exp(-jnp.abs(x)))


ADA_TN = 1024
ADA_RC = 64


def _ada_kernel(c_ref, w_ref, b_ref, o_ref):
    def body(r, acc):
        rows = pl.ds(pl.multiple_of(r * ADA_RC, ADA_RC), ADA_RC)
        cond = _silu(c_ref[rows, :])
        blk = w_ref[rows, :] * cond
        return acc + jnp.sum(blk.reshape(ADA_RC // 8, 8, ADA_TN), axis=0)

    acc = lax.fori_loop(0, D_MODEL // ADA_RC, body, jnp.zeros((8, ADA_TN), F32))
    o_ref[...] = jnp.sum(acc, axis=0, keepdims=True) + b_ref[...]


def _ada_mod(c, w_ada, b_ada):
    n = w_ada.shape[1]
    return pl.pallas_call(
        _ada_kernel,
        grid=(n // ADA_TN,),
        in_specs=[
            pl.BlockSpec((D_MODEL, 1), lambda j: (0, 0)),
            pl.BlockSpec((D_MODEL, ADA_TN), lambda j: (0, j)),
            pl.BlockSpec((1, ADA_TN), lambda j: (0, j)),
        ],
        out_specs=pl.BlockSpec((1, ADA_TN), lambda j: (0, j)),
        out_shape=jax.ShapeDtypeStruct((1, n), F32),
        compiler_params=_cparams(("arbitrary",)),
        name="ada_mod",
    )(c.reshape(D_MODEL, 1), w_ada, b_ada.reshape(1, n))


INP_TM = 1024
INP_TN = 1024


def _inproj_kernel(x_ref, sc_ref, sh_ref, w_ref, wg_ref, o_ref, og_ref, u_scr):
    @pl.when(pl.program_id(1) == 0)
    def _():
        u = _layer_norm(x_ref[...]) * (1.0 + sc_ref[...]) + sh_ref[...]
        ub = u.astype(BF16)
        u_scr[...] = ub
        og_ref[...] = _dot(ub, wg_ref[...])

    o_ref[...] = _dot(u_scr[...], w_ref[...])


def _inproj(x, scale, shift, w_main, w_gate):
    t = x.shape[0]
    tm = min(INP_TM, t)
    return pl.pallas_call(
        _inproj_kernel,
        grid=(t // tm, N_MAIN // INP_TN),
        in_specs=[
            pl.BlockSpec((tm, D_MODEL), lambda i, j: (i, 0)),
            pl.BlockSpec((1, D_MODEL), lambda i, j: (0, 0)),
            pl.BlockSpec((1, D_MODEL), lambda i, j: (0, 0)),
            pl.BlockSpec((D_MODEL, INP_TN), lambda i, j: (0, j)),
            pl.BlockSpec((D_MODEL, GATE_PAD), lambda i, j: (0, 0)),
        ],
        out_specs=[
            pl.BlockSpec((tm, INP_TN), lambda i, j: (i, j)),
            pl.BlockSpec((tm, GATE_PAD), lambda i, j: (i, 0)),
        ],
        out_shape=[
            jax.ShapeDtypeStruct((t, N_MAIN), F32),
            jax.ShapeDtypeStruct((t, GATE_PAD), F32),
        ],
        scratch_shapes=[pltpu.VMEM((tm, D_MODEL), BF16)],
        compiler_params=_cparams(("arbitrary", "arbitrary")),
        name="ln_inproj",
    )(x, scale, shift, w_main, w_gate)


GP_TM = 512


def _split3(x):
    hi = x.astype(BF16)
    r1 = x - hi.astype(F32)
    mid = r1.astype(BF16)
    lo = (r1 - mid.astype(F32)).astype(BF16)
    return hi, mid, lo


def _tri_dot(tri, x):
    hi, mid, lo = _split3(x)
    return _dot(tri, hi) + _dot(tri, mid) + _dot(tri, lo)


def _gateprep_kernel(g_ref, bias_ref, alog_ref, o_ref):
    tm = g_ref.shape[0]
    lane = lax.broadcasted_iota(jnp.int32, (CHUNK, GATE_PAD), 1)
    ii = lax.broadcasted_iota(jnp.int32, (CHUNK, CHUNK), 0)
    jj = lax.broadcasted_iota(jnp.int32, (CHUNK, CHUNK), 1)
    tril = jnp.where(jj <= ii, 1.0, 0.0).astype(BF16)
    triu = jnp.where(jj >= ii, 1.0, 0.0).astype(BF16)
    neg_a = -jnp.exp(alog_ref[...])
    is_i = lane < COL_F
    is_f = (lane >= COL_F) & (lane < COL_B)
    is_b = (lane >= COL_B) & (lane < COL_A)
    fwd_cum = ((lane >= COL_F) & (lane < COL_F + M_HEADS)) | ((lane >= COL_A) & (lane < COL_A + G_HEADS))
    bwd_cum = ((lane >= COL_F + M_HEADS) & (lane < COL_B)) | ((lane >= COL_A + G_HEADS) & (lane < N_GATE))
    for c in range(tm // CHUNK):
        rows = slice(c * CHUNK, (c + 1) * CHUNK)
        x = g_ref[rows, :] + bias_ref[...]
        cap = GATE_SOFTCAP * jnp.tanh(x / GATE_SOFTCAP)
        log_f = jnp.minimum(cap, 0.0) - jnp.log1p(jnp.exp(-jnp.abs(cap)))
        beta = _sigmoid(x)
        decay = neg_a * _softplus(x)
        act = jnp.where(is_i, cap, jnp.where(is_f, log_f, jnp.where(is_b, beta, decay)))
        cum_f = _tri_dot(tril, act)
        cum_b = _tri_dot(triu, act)
        o_ref[rows, :] = jnp.where(fwd_cum, cum_f, jnp.where(bwd_cum, cum_b, act))


def _gateprep(graw, bias_row, alog_row):
    t = graw.shape[0]
    tm = min(GP_TM, t)
    return pl.pallas_call(
        _gateprep_kernel,
        grid=(t // tm,),
        in_specs=[
            pl.BlockSpec((tm, GATE_PAD), lambda i: (i, 0)),
            pl.BlockSpec((1, GATE_PAD), lambda i: (0, 0)),
            pl.BlockSpec((1, GATE_PAD), lambda i: (0, 0)),
        ],
        out_specs=pl.BlockSpec((tm, GATE_PAD), lambda i: (i, 0)),
        out_shape=jax.ShapeDtypeStruct((t, GATE_PAD), F32),
        compiler_params=_cparams(("arbitrary",)),
        name="gate_prep",
    )(graw, bias_row, alog_row)


CV_TM = 256
HALO = 8


def _conv_kernel(cur_ref, prev_ref, next_ref, cw_ref, o_ref):
    i = pl.program_id(0)
    n = pl.num_programs(0)
    tm = cur_ref.shape[0]
    pad = CONV_WIDTH // 2
    keep_prev = jnp.where(i > 0, 1.0, 0.0)
    keep_next = jnp.where(i < n - 1, 1.0, 0.0)
    for cb in range(3 * G_HEADS):
        cols = slice(cb * G_HEAD_DIM, (cb + 1) * G_HEAD_DIM)
        xp = jnp.concatenate(
            [prev_ref[:, cols] * keep_prev, cur_ref[:, cols], next_ref[:, cols] * keep_next], axis=0)
        acc = jnp.zeros((tm, G_HEAD_DIM), F32)
        for w in range(CONV_WIDTH):
            lo = HALO - pad + w
            acc = acc + xp[lo:lo + tm, :] * cw_ref[w:w + 1, cols]
        y = _silu(acc)
        if cb < 2 * G_HEADS:
            y = y * lax.rsqrt(jnp.sum(y * y, axis=-1, keepdims=True) + NORM_EPS)
            if cb < G_HEADS:
                y = y * (G_HEAD_DIM ** -0.5)
        o_ref[:, cols] = y


def _gdn_conv(proj, conv_w):
    t = proj.shape[0]
    tm = min(CV_TM, t)
    nh = tm // HALO
    nb = t // HALO
    c3 = 3 * G_W
    cblk = OFF_GQKV // c3
    return pl.pallas_call(
        _conv_kernel,
        grid=(t // tm,),
        in_specs=[
            pl.BlockSpec((tm, c3), lambda i: (i, cblk)),
            pl.BlockSpec((HALO, c3), lambda i: (jnp.maximum(i * nh - 1, 0), cblk)),
            pl.BlockSpec((HALO, c3), lambda i: (jnp.minimum((i + 1) * nh, nb - 1), cblk)),
            pl.BlockSpec((8, c3), lambda i: (0, 0)),
        ],
        out_specs=pl.BlockSpec((tm, c3), lambda i: (i, 0)),
        out_shape=jax.ShapeDtypeStruct((t, c3), F32),
        compiler_params=_cparams(("arbitrary",)),
        name="gdn_conv",
    )(proj, proj, proj, conv_w)


SCAN_RB = 512


def _tri_masks(reverse):
    ii = lax.broadcasted_iota(jnp.int32, (CHUNK, CHUNK), 0)
    jj = lax.broadcasted_iota(jnp.int32, (CHUNK, CHUNK), 1)
    if reverse:
        return jj >= ii, jj > ii
    return jj <= ii, jj < ii


def _mlstm_chunks(chains):
    for ch in chains:
        incl, _ = _tri_masks(ch["reverse"])
        ch["d"] = jnp.where(incl, ch["bc_c"] - ch["bc_r"] + ch["ic_r"], -jnp.inf)
        ch["inter"] = ch["bc_c"] + ch["m"]
        ch["qb"] = ch["q"].astype(BF16)
        ch["vb"] = ch["v"].astype(BF16)
    for ch in chains:
        ch["qk"] = _dot_nt(ch["qb"], ch["k"].astype(BF16))
    for ch in chains:
        ch["rmax"] = jnp.max(ch["d"], axis=1, keepdims=True)
    for ch in chains:
        ch["qc"] = _dot(ch["qb"], ch["c"].astype(BF16))
    for ch in chains:
        ch["qn"] = jnp.sum(ch["q"] * ch["n"], axis=1, keepdims=True)
    for ch in chains:
        bc_c = ch["bc_c"]
        b_last = bc_c[0:1, :] if ch["reverse"] else bc_c[CHUNK - 1:CHUNK, :]
        w_log = b_last - bc_c + ch["ic_c"]
        ch["m_new"] = jnp.maximum(b_last + ch["m"], jnp.max(w_log, axis=0, keepdims=True))
        ch["carry"] = jnp.exp(b_last + ch["m"] - ch["m_new"])
        ch["kw"] = ch["k"] * jnp.exp(w_log - ch["m_new"])
    for ch in chains:
        ch["kv"] = _dot_tn(ch["kw"].astype(BF16), ch["vb"])
    for ch in chains:
        ch["m_row"] = jnp.maximum(ch["rmax"], ch["inter"])
        ch["s_inter"] = jnp.exp(ch["inter"] - ch["m_row"])
        ch["w"] = jnp.exp(ch["d"] - ch["m_row"]) * ch["qk"]
    for ch in chains:
        ch["wv"] = _dot(ch["w"].astype(BF16), ch["vb"])
    for ch in chains:
        ch["wsum"] = jnp.sum(ch["w"], axis=1, keepdims=True)
    for ch in chains:
        ch["c_new"] = ch["carry"] * ch["c"] + ch["kv"]
        ch["n_new"] = ch["carry"] * ch["n"] + jnp.sum(ch["kw"], axis=0, keepdims=True)
    for ch in chains:
        num = ch["wv"] + ch["s_inter"] * ch["qc"]
        den = ch["wsum"] + ch["s_inter"] * ch["qn"]
        ch["out"] = num / jnp.maximum(jnp.abs(den), jnp.exp(-ch["m_row"]))


def _mlstm_kernel(qf_ref, kf_ref, vf_ref, gcf_ref, grf_ref,
                  qb_ref, kb_ref, vb_ref, gcb_ref, grb_ref,
                  of_ref, ob_ref, c_scr, n_scr, m_scr):
    @pl.when(pl.program_id(0) == 0)
    def _():
        c_scr[...] = jnp.zeros_like(c_scr)
        n_scr[...] = jnp.zeros_like(n_scr)
        m_scr[...] = jnp.zeros_like(m_scr)

    ncb = qf_ref.shape[0] // CHUNK
    kscale = M_QK_DIM ** -0.5

    def body(c, carry):
        chains = []
        for reverse in (False, True):
            cc = (ncb - 1 - c) if reverse else c
            rows = pl.ds(pl.multiple_of(cc * CHUNK, CHUNK), CHUNK)
            q_ref, k_ref, v_ref, gc_ref, gr_ref, o_ref = (
                (qb_ref, kb_ref, vb_ref, gcb_ref, grb_ref, ob_ref) if reverse
                else (qf_ref, kf_ref, vf_ref, gcf_ref, grf_ref, of_ref))
            gc = gc_ref[rows, :]
            gr = gr_ref[cc]
            for h in range(M_HEADS):
                s = (1 if reverse else 0) * M_HEADS + h
                ci, cf = COL_I + s, COL_F + s
                chains.append(dict(
                    reverse=reverse, s=s, o_ref=o_ref, rows=rows, h=h,
                    q=q_ref[rows, h * M_QK_DIM:(h + 1) * M_QK_DIM],
                    k=k_ref[rows, h * M_QK_DIM:(h + 1) * M_QK_DIM] * kscale,
                    v=v_ref[rows, h * M_V_DIM:(h + 1) * M_V_DIM],
                    ic_c=gc[:, ci:ci + 1], bc_c=gc[:, cf:cf + 1],
                    ic_r=gr[ci:ci + 1, :], bc_r=gr[cf:cf + 1, :],
                    c=c_scr[s], n=n_scr[s], m=m_scr[s][:, 0:1]))
        _mlstm_chunks(chains)
        for ch in chains:
            h, s = ch["h"], ch["s"]
            ch["o_ref"][ch["rows"], h * M_V_DIM:(h + 1) * M_V_DIM] = ch["out"]
            c_scr[s] = ch["c_new"]
            n_scr[s] = ch["n_new"]
            m_scr[s] = jnp.broadcast_to(ch["m_new"], (1, 128))
        return carry

    lax.fori_loop(0, ncb, body, 0)


def _mlstm(proj, g_col, g_row):
    t = proj.shape[0]
    rb = min(SCAN_RB, t)
    nb = t // rb
    ncb = rb // CHUNK
    fwd = lambda s: s
    bwd = lambda s: nb - 1 - s

    def specs(rmap):
        return [
            pl.BlockSpec((rb, M_QK), lambda s: (rmap(s), OFF_MQ // M_QK)),
            pl.BlockSpec((rb, M_QK), lambda s: (rmap(s), OFF_MK // M_QK)),
            pl.BlockSpec((rb, M_V), lambda s: (rmap(s), OFF_MV // M_V)),
            pl.BlockSpec((rb, GATE_PAD), lambda s: (rmap(s), 0)),
            pl.BlockSpec((ncb, GATE_PAD, CHUNK), lambda s: (rmap(s), 0, 0)),
        ]

    ns = 2 * M_HEADS
    return pl.pallas_call(
        _mlstm_kernel,
        grid=(nb,),
        in_specs=specs(fwd) + specs(bwd),
        out_specs=[pl.BlockSpec((rb, M_V), lambda s: (s, 0)),
                   pl.BlockSpec((rb, M_V), lambda s: (nb - 1 - s, 0))],
        out_shape=[jax.ShapeDtypeStruct((t, M_V), F32)] * 2,
        scratch_shapes=[pltpu.VMEM((ns, M_QK_DIM, M_V_DIM), F32),
                        pltpu.VMEM((ns, 1, M_QK_DIM), F32),
                        pltpu.VMEM((ns, 1, 128), F32)],
        compiler_params=_cparams(("arbitrary",)),
        name="mlstm_scan",
    )(proj, proj, proj, g_col, g_row, proj, proj, proj, g_col, g_row)


GDN_CHUNKS_PER_ITER = 2


def _gdn_local(chains):
    ii = lax.broadcasted_iota(jnp.int32, (CHUNK, CHUNK), 0)
    jj = lax.broadcasted_iota(jnp.int32, (CHUNK, CHUNK), 1)
    eye = jnp.where(ii == jj, 1.0, 0.0)
    for ch in chains:
        incl, strict = _tri_masks(ch["reverse"])
        gc_c = ch["gc_c"]
        ch["strict"] = strict
        ch["decay"] = jnp.exp(jnp.where(incl, gc_c - ch["gc_r"], -jnp.inf))
        ch["kb16"] = ch["k"].astype(BF16)
        ch["kbeta"] = ch["k"] * ch["beta_c"]
        g_last = gc_c[0:1, :] if ch["reverse"] else gc_c[CHUNK - 1:CHUNK, :]
        ch["s_decay"] = jnp.exp(g_last)
        ch["kdec"] = (ch["k"] * jnp.exp(g_last - gc_c)).astype(BF16)
        ch["qg"] = (ch["q"] * jnp.exp(gc_c)).astype(BF16)
    for ch in chains:
        ch["kk"] = _dot_nt(ch["kbeta"].astype(BF16), ch["kb16"])
    for ch in chains:
        ch["attn"] = (_dot_nt(ch["q"].astype(BF16), ch["kb16"]) * ch["decay"]).astype(BF16)
    for ch in chains:
        ch["pw"] = -jnp.where(ch["strict"], ch["kk"] * ch["decay"], 0.0)
        ch["inv"] = eye + ch["pw"]
    for _ in range(5):
        for ch in chains:
            pwb = ch["pw"].astype(BF16)
            ch["pw"] = _dot(pwb, pwb)
        for ch in chains:
            ch["inv"] = ch["inv"] + _dot(ch["inv"].astype(BF16), ch["pw"].astype(BF16))
    for ch in chains:
        rhs = jnp.concatenate([ch["v"] * ch["beta_c"], ch["kbeta"] * jnp.exp(ch["gc_c"])], axis=1)
        sol = _dot(ch["inv"].astype(BF16), rhs.astype(BF16))
        ch["u"] = sol[:, :G_HEAD_DIM]
        ch["w"] = sol[:, G_HEAD_DIM:].astype(BF16)


def _gdn_state(chains):
    for ch in chains:
        ch["sb"] = ch["s"].astype(BF16)
    for ch in chains:
        ch["vnb"] = (ch["u"] - _dot(ch["w"], ch["sb"])).astype(BF16)
    for ch in chains:
        ch["qs"] = _dot(ch["qg"], ch["sb"])
    for ch in chains:
        ch["s_new"] = ch["s"] * ch["s_decay"] + _dot_tn(ch["kdec"], ch["vnb"])
    for ch in chains:
        ch["out"] = ch["qs"] + _dot(ch["attn"], ch["vnb"])


def _gdn_kernel(qf_ref, kf_ref, vf_ref, gcf_ref, grf_ref,
                qb_ref, kb_ref, vb_ref, gcb_ref, grb_ref,
                of_ref, ob_ref, s_scr):
    @pl.when(pl.program_id(0) == 0)
    def _():
        s_scr[...] = jnp.zeros_like(s_scr)

    ncb = qf_ref.shape[0] // CHUNK

    def load_chains(c):
        chains = []
        for reverse in (False, True):
            cc = (ncb - 1 - c) if reverse else c
            rows = pl.ds(pl.multiple_of(cc * CHUNK, CHUNK), CHUNK)
            q_ref, k_ref, v_ref, gc_ref, gr_ref, o_ref = (
                (qb_ref, kb_ref, vb_ref, gcb_ref, grb_ref, ob_ref) if reverse
                else (qf_ref, kf_ref, vf_ref, gcf_ref, grf_ref, of_ref))
            gc = gc_ref[rows, :]
            gr = gr_ref[cc]
            for h in range(G_HEADS):
                s = (1 if reverse else 0) * G_HEADS + h
                cb, ca = COL_B + s, COL_A + s
                cols = slice(h * G_HEAD_DIM, (h + 1) * G_HEAD_DIM)
                chains.append(dict(
                    reverse=reverse, slot=s, o_ref=o_ref, rows=rows, cols=cols,
                    q=q_ref[rows, cols], k=k_ref[rows, cols], v=v_ref[rows, cols],
                    beta_c=gc[:, cb:cb + 1], gc_c=gc[:, ca:ca + 1], gc_r=gr[ca:ca + 1, :]))
        return chains

    def body(p, carry):
        groups = [load_chains(p * GDN_CHUNKS_PER_ITER + i) for i in range(GDN_CHUNKS_PER_ITER)]
        _gdn_local([ch for g in groups for ch in g])
        state = [s_scr[s] for s in range(2 * G_HEADS)]
        for g in groups:
            for ch in g:
                ch["s"] = state[ch["slot"]]
            _gdn_state(g)
            for ch in g:
                ch["o_ref"][ch["rows"], ch["cols"]] = ch["out"]
                state[ch["slot"]] = ch["s_new"]
        for s in range(2 * G_HEADS):
            s_scr[s] = state[s]
        return carry

    lax.fori_loop(0, ncb // GDN_CHUNKS_PER_ITER, body, 0)


def _gdn(qkv, g_col, g_row):
    t = qkv.shape[0]
    rb = min(SCAN_RB, t)
    nb = t // rb
    ncb = rb // CHUNK
    fwd = lambda s: s
    bwd = lambda s: nb - 1 - s

    def specs(rmap):
        return [
            pl.BlockSpec((rb, G_W), lambda s: (rmap(s), 0)),
            pl.BlockSpec((rb, G_W), lambda s: (rmap(s), 1)),
            pl.BlockSpec((rb, G_W), lambda s: (rmap(s), 2)),
            pl.BlockSpec((rb, GATE_PAD), lambda s: (rmap(s), 0)),
            pl.BlockSpec((ncb, GATE_PAD, CHUNK), lambda s: (rmap(s), 0, 0)),
        ]

    return pl.pallas_call(
        _gdn_kernel,
        grid=(nb,),
        in_specs=specs(fwd) + specs(bwd),
        out_specs=[pl.BlockSpec((rb, G_W), lambda s: (s, 0)),
                   pl.BlockSpec((rb, G_W), lambda s: (nb - 1 - s, 0))],
        out_shape=[jax.ShapeDtypeStruct((t, G_W), F32)] * 2,
        scratch_shapes=[pltpu.VMEM((2 * G_HEADS, G_HEAD_DIM, G_HEAD_DIM), F32)],
        compiler_params=_cparams(("arbitrary",)),
        name="gdn_scan",
    )(qkv, qkv, qkv, g_col, g_row, qkv, qkv, qkv, g_col, g_row)


OP_TM = 256
OP_SUB = 128


def _outproj_kernel(mf_ref, mb_ref, gf_ref, gb_ref, mo_ref, gz_ref, x_ref, mod_ref,
                    mnw_ref, gnw_ref, wout_ref, ln_ref, rw_ref,
                    h1_ref, u2_ref, lg_ref):
    gate1, scale2, shift2 = mod_ref[0:1, :], mod_ref[1:2, :], mod_ref[2:3, :]
    subs = [slice(r, r + OP_SUB) for r in range(0, x_ref.shape[0], OP_SUB)]
    mixed, y, h1s, u2s = {}, {}, {}, {}
    for i, rows in enumerate(subs):
        hm = mf_ref[rows, :] + mb_ref[rows, :]
        hg = gf_ref[rows, :] + gb_ref[rows, :]
        parts = []
        for h in range(M_HEADS):
            seg = hm[:, h * M_V_DIM:(h + 1) * M_V_DIM]
            parts.append(seg * lax.rsqrt(jnp.mean(seg * seg, axis=-1, keepdims=True) + NORM_EPS))
        hm_n = jnp.concatenate(parts, axis=1) * mnw_ref[...] * _sigmoid(mo_ref[rows, :])
        parts = []
        for h in range(G_HEADS):
            seg = hg[:, h * G_HEAD_DIM:(h + 1) * G_HEAD_DIM]
            parts.append(seg * lax.rsqrt(jnp.mean(seg * seg, axis=-1, keepdims=True) + NORM_EPS))
        hg_n = jnp.concatenate(parts, axis=1) * gnw_ref[...] * _silu(gz_ref[rows, :])
        mixed[i] = jnp.concatenate([hm_n, hg_n], axis=1).astype(BF16)
    for i, rows in enumerate(subs):
        y[i] = _dot(mixed[i], wout_ref[...])
    for i, rows in enumerate(subs):
        h1 = (_layer_norm(DEEPNORM_ALPHA * x_ref[rows, :] + (1.0 + gate1) * y[i]) * ln_ref[0:1, :]
              + ln_ref[1:2, :])
        h1_ref[rows, :] = h1
        h1s[i] = h1
    for i, rows in enumerate(subs):
        u2 = _layer_norm(h1s[i]) * (1.0 + scale2) + shift2
        u2_ref[rows, :] = u2
        u2s[i] = u2
    for i, rows in enumerate(subs):
        lg_ref[:, rows] = lax.dot_general(rw_ref[...], u2s[i], (((1,), (1,)), ((), ())),
                                          precision=lax.Precision.HIGHEST, preferred_element_type=F32)


def _outproj(mf, mb, gf, gb, proj, x, mod3, mnw, gnw, w_out, ln1, rw_t):
    t = x.shape[0]
    tm = min(OP_TM, t)
    row = lambda i: (i, 0)
    const = lambda i: (0, 0)
    return pl.pallas_call(
        _outproj_kernel,
        grid=(t // tm,),
        in_specs=[
            pl.BlockSpec((tm, M_V), row), pl.BlockSpec((tm, M_V), row),
            pl.BlockSpec((tm, G_W), row), pl.BlockSpec((tm, G_W), row),
            pl.BlockSpec((tm, M_V), lambda i: (i, OFF_MO // M_V)),
            pl.BlockSpec((tm, G_W), lambda i: (i, OFF_GZ // G_W)),
            pl.BlockSpec((tm, D_MODEL), row),
            pl.BlockSpec((8, D_MODEL), const),
            pl.BlockSpec((1, M_V), const), pl.BlockSpec((1, G_W), const),
            pl.BlockSpec((D_MODEL, D_MODEL), const),
            pl.BlockSpec((8, D_MODEL), const),
            pl.BlockSpec((N_EXPERTS, D_MODEL), const),
        ],
        out_specs=[
            pl.BlockSpec((tm, D_MODEL), row),
            pl.BlockSpec((tm, D_MODEL), row),
            pl.BlockSpec((N_EXPERTS, tm), lambda i: (0, i)),
        ],
        out_shape=[
            jax.ShapeDtypeStruct((t, D_MODEL), F32),
            jax.ShapeDtypeStruct((t, D_MODEL), F32),
            jax.ShapeDtypeStruct((N_EXPERTS, t), F32),
        ],
        compiler_params=_cparams(("arbitrary",)),
        name="outproj_ln",
    )(mf, mb, gf, gb, proj, proj, x, mod3, mnw, gnw, w_out, ln1, rw_t)


RT_TN = 512
GROUP_SIZE = N_EXPERTS // N_GROUPS


def _route_kernel(lg_ref, bias_ref, idx_ref, w_ref, rank_ref, cnt_ref, carry_scr):
    tn = lg_ref.shape[1]

    @pl.when(pl.program_id(0) == 0)
    def _():
        carry_scr[...] = jnp.zeros_like(carry_scr)

    neg = -jnp.inf
    scores = _sigmoid(lg_ref[...])
    biased = scores + bias_ref[...]
    sub8 = lax.broadcasted_iota(jnp.int32, (GROUP_SIZE, tn), 0).astype(F32)
    gscore = []
    for g in range(N_GROUPS):
        bg = biased[g * GROUP_SIZE:(g + 1) * GROUP_SIZE, :]
        m1 = jnp.max(bg, axis=0, keepdims=True)
        first = jnp.min(jnp.where(bg == m1, sub8, float(GROUP_SIZE)), axis=0, keepdims=True)
        m2 = jnp.max(jnp.where(sub8 == first, neg, bg), axis=0, keepdims=True)
        gscore.append(m1 + m2)
    masked = []
    for g in range(N_GROUPS):
        beaten = jnp.zeros((1, tn), F32)
        for g2 in range(N_GROUPS):
            if g2 == g:
                continue
            wins = (gscore[g2] >= gscore[g]) if g2 < g else (gscore[g2] > gscore[g])
            beaten = beaten + jnp.where(wins, 1.0, 0.0)
        keep = beaten < float(TOPK_GROUPS)
        masked.append(jnp.where(keep, biased[g * GROUP_SIZE:(g + 1) * GROUP_SIZE, :], neg))
    x = jnp.concatenate(masked, axis=0)
    eidx = lax.broadcasted_iota(jnp.int32, (N_EXPERTS, tn), 0).astype(F32)
    member = jnp.zeros((N_EXPERTS, tn), F32)
    sel_idx, sel_s = [], []
    for _ in range(TOP_K):
        m = jnp.max(x, axis=0, keepdims=True)
        idx = jnp.min(jnp.where(x == m, eidx, float(N_EXPERTS)), axis=0, keepdims=True)
        sel = eidx == idx
        sel_idx.append(idx)
        sel_s.append(jnp.sum(jnp.where(sel, scores, 0.0), axis=0, keepdims=True))
        member = member + jnp.where(sel, 1.0, 0.0)
        x = jnp.where(sel, neg, x)
    total = sel_s[0]
    for s in sel_s[1:]:
        total = total + s
    ti = lax.broadcasted_iota(jnp.int32, (tn, tn), 0)
    tj = lax.broadcasted_iota(jnp.int32, (tn, tn), 1)
    before = jnp.where(ti < tj, 1.0, 0.0).astype(BF16)
    carry = carry_scr[:, 0:1]
    excl = _dot(member.astype(BF16), before) + carry
    ranks = [jnp.sum(jnp.where(eidx == i, excl, 0.0), axis=0, keepdims=True) for i in sel_idx]
    zero = jnp.zeros((1, tn), F32)
    idx_ref[...] = jnp.concatenate(sel_idx + [zero, zero], axis=0).astype(jnp.int32)
    w_ref[...] = jnp.concatenate([ROUTED_SCALE * s / total for s in sel_s] + [zero, zero], axis=0)
    rank_ref[...] = jnp.concatenate(ranks + [zero, zero], axis=0).astype(jnp.int32)
    new_carry = carry + jnp.sum(member, axis=1, keepdims=True)
    carry_scr[...] = jnp.broadcast_to(new_carry, carry_scr.shape)
    cnt_ref[...] = jnp.broadcast_to(new_carry, cnt_ref.shape).astype(jnp.int32)


def _route(logits_t, bias_col):
    t = logits_t.shape[1]
    tn = min(RT_TN, t)
    col = lambda i: (0, i)
    return pl.pallas_call(
        _route_kernel,
        grid=(t // tn,),
        in_specs=[pl.BlockSpec((N_EXPERTS, tn), col), pl.BlockSpec((N_EXPERTS, 1), lambda i: (0, 0))],
        out_specs=[pl.BlockSpec((8, tn), col), pl.BlockSpec((8, tn), col), pl.BlockSpec((8, tn), col),
                   pl.BlockSpec((N_EXPERTS, 128), lambda i: (0, 0))],
        out_shape=[jax.ShapeDtypeStruct((8, t), jnp.int32), jax.ShapeDtypeStruct((8, t), F32),
                   jax.ShapeDtypeStruct((8, t), jnp.int32), jax.ShapeDtypeStruct((N_EXPERTS, 128), jnp.int32)],
        scratch_shapes=[pltpu.VMEM((N_EXPERTS, 128), F32)],
        compiler_params=_cparams(("arbitrary",)),
        name="route_topk",
    )(logits_t, bias_col)


GMM_PAIRS = 2
GMM_HALF = EXPERT_BLOCK // 2


def _gmm_kernel(be_ref, nb_ref, first_ref, slot_ref, nxt_ref, half_ref, x_ref, w_hbm, *rest, mode):
    wbuf, sem = rest[-2], rest[-1]
    o_ref = rest[-3]
    tb = EXPERT_BLOCK
    nb = nb_ref[0]

    def weight_copy(e, s):
        return pltpu.make_async_copy(w_hbm.at[e], wbuf.at[s], sem.at[s])

    def enter_run(b):
        @pl.when(first_ref[b] == 1)
        def _():
            weight_copy(be_ref[b], slot_ref[b]).wait()

            @pl.when(nxt_ref[b] >= 0)
            def _():
                weight_copy(nxt_ref[b], 1 - slot_ref[b]).start()

    def zero(r0, nrows):
        o_ref[pl.ds(r0, nrows), :] = jnp.zeros((nrows, o_ref.shape[1]), o_ref.dtype)

    def compute(r0, nrows, b):
        rows = pl.ds(r0, nrows)
        acc = _dot(x_ref[rows, :].astype(BF16), wbuf[slot_ref[b]].astype(BF16))
        if mode == "silu":
            acc = _silu(acc)
        elif mode == "mul":
            acc = acc * rest[0][rows, :].astype(F32)
        o_ref[rows, :] = acc.astype(o_ref.dtype)

    def trimmed(r0, nrows, b_run, b_last):
        @pl.when(half_ref[b_last] == 0)
        def _():
            compute(r0, nrows, b_run)

        @pl.when(half_ref[b_last] == 1)
        def _():
            compute(r0, nrows - GMM_HALF, b_run)
            zero(r0 + nrows - GMM_HALF, GMM_HALF)

    def block(r0, b):
        @pl.when(b < nb)
        def _():
            trimmed(r0, tb, b, b)

        @pl.when(b >= nb)
        def _():
            zero(r0, tb)

    @pl.when(pl.program_id(0) == 0)
    def _():
        weight_copy(be_ref[0], 0).start()

    def pair_body(pair, carry):
        b0 = (pl.program_id(0) * GMM_PAIRS + pair) * 2
        b1 = b0 + 1
        r0 = pl.multiple_of(pair * 2 * tb, 2 * tb)
        enter_run(b0)
        same_run = (b1 < nb) & (first_ref[b1] == 0)

        @pl.when(same_run)
        def _():
            trimmed(r0, 2 * tb, b0, b1)

        @pl.when(jnp.logical_not(same_run))
        def _():
            block(r0, b0)
            enter_run(b1)
            block(r0 + tb, b1)

        return carry

    lax.fori_loop(0, GMM_PAIRS, pair_body, 0)


def _run_tables(block_e, n_used):
    nb = block_e.shape[0]
    idx = jnp.arange(nb, dtype=jnp.int32)
    used = idx < n_used[0]
    prev = jnp.concatenate([jnp.full((1,), -1, jnp.int32), block_e[:-1]])
    first = ((block_e != prev) & used).astype(jnp.int32)
    slot = (jnp.cumsum(first) - 1) & 1
    later_first = jnp.where(first == 1, idx, nb)
    nxt_pos = jnp.flip(lax.cummin(jnp.flip(jnp.concatenate([later_first[1:], jnp.full((1,), nb, jnp.int32)]))))
    nxt = jnp.where(nxt_pos < nb, jnp.take(block_e, jnp.minimum(nxt_pos, nb - 1)), -1)
    return first, slot.astype(jnp.int32), nxt.astype(jnp.int32)


def _gmm(x, w, tables, extra, mode, out_dtype):
    block_e, n_used, first, slot, nxt, half = tables
    rows, k = x.shape
    n = w.shape[2]
    bps = 2 * GMM_PAIRS
    tb = bps * EXPERT_BLOCK
    xmap = lambda p, be, nb, fi, sl, nx, hf: (jnp.minimum(p, (nb[0] - 1) // bps), 0)
    in_specs = [pl.BlockSpec((tb, k), xmap), pl.BlockSpec(memory_space=pl.ANY)]
    args = [x, w]
    if mode == "mul":
        in_specs.append(pl.BlockSpec((tb, n), xmap))
        args.append(extra)
    return pl.pallas_call(
        functools.partial(_gmm_kernel, mode=mode),
        grid_spec=pltpu.PrefetchScalarGridSpec(
            num_scalar_prefetch=6,
            grid=(rows // tb,),
            in_specs=in_specs,
            out_specs=pl.BlockSpec((tb, n), lambda p, be, nb, fi, sl, nx, hf: (p, 0)),
            scratch_shapes=[pltpu.VMEM((2, k, n), F32), pltpu.SemaphoreType.DMA((2,))],
        ),
        out_shape=jax.ShapeDtypeStruct((rows, n), out_dtype),
        compiler_params=_cparams(("arbitrary",)),
        name="gmm_" + mode,
    )(block_e, n_used, first, slot, nxt, half, *args)


def _swiglu_grouped(x, wg, wu, wd, tables):
    a = _gmm(x, wg, tables, None, "silu", BF16)
    h = _gmm(x, wu, tables, a, "mul", BF16)
    return _gmm(h, wd, tables, None, "plain", F32)


DSP_TT = 256
ROW_UNROLL = 8


def _dispatch_kernel(zf_ref, dest_ref, u_ref, xs_ref, zero_scr, sem, zsem):
    tt = u_ref.shape[0]

    @pl.when(pl.program_id(0) == 0)
    def _():
        zero_scr[...] = jnp.zeros_like(zero_scr)

        def block_fill(b):
            rows = pl.ds(pl.multiple_of(b * EXPERT_BLOCK, EXPERT_BLOCK), EXPERT_BLOCK)
            return pltpu.make_async_copy(zero_scr, xs_ref.at[rows], zsem)

        def fill_start(b, carry):
            @pl.when(zf_ref[b] != 0)
            def _():
                block_fill(b).start()
            return carry

        def fill_wait(b, carry):
            @pl.when(zf_ref[b] != 0)
            def _():
                block_fill(b).wait()
            return carry

        lax.fori_loop(0, zf_ref.shape[0], fill_start, 0)
        lax.fori_loop(0, zf_ref.shape[0], fill_wait, 0)

    def row_copy(t, k):
        return pltpu.make_async_copy(u_ref.at[pl.ds(t, 1)], xs_ref.at[pl.ds(dest_ref[k, t], 1)], sem)

    _row_dma_loops(tt, row_copy)


def _row_dma_loops(tt, row_copy):
    def issue(g, carry):
        t0 = pl.multiple_of(g * ROW_UNROLL, ROW_UNROLL)
        for r in range(ROW_UNROLL):
            for k in range(TOP_K):
                row_copy(t0 + r, k).start(priority=k % 2)
        return carry

    def drain(g, carry):
        t0 = pl.multiple_of(g * ROW_UNROLL, ROW_UNROLL)
        for r in range(ROW_UNROLL):
            for k in range(TOP_K):
                row_copy(t0 + r, k).wait()
        return carry

    lax.fori_loop(0, tt // ROW_UNROLL, issue, 0)
    lax.fori_loop(0, tt // ROW_UNROLL, drain, 0)


def _dispatch(u2, dest, zero_flag, n_slots):
    t = u2.shape[0]
    tt = min(DSP_TT, t)
    return pl.pallas_call(
        _dispatch_kernel,
        grid_spec=pltpu.PrefetchScalarGridSpec(
            num_scalar_prefetch=1,
            grid=(t // tt,),
            in_specs=[pl.BlockSpec((8, tt), lambda i, zf: (0, i), memory_space=pltpu.SMEM),
                      pl.BlockSpec((tt, D_MODEL), lambda i, zf: (i, 0))],
            out_specs=pl.BlockSpec(memory_space=pl.ANY),
            scratch_shapes=[pltpu.VMEM((EXPERT_BLOCK, D_MODEL), F32),
                            pltpu.SemaphoreType.DMA(()), pltpu.SemaphoreType.DMA(())],
        ),
        out_shape=jax.ShapeDtypeStruct((n_slots, D_MODEL), F32),
        compiler_params=_cparams(("arbitrary",)),
        name="moe_dispatch",
    )(zero_flag, dest, u2)


CMB_TT = 128


def _combine_kernel(dest_ref, ys_ref, w_ref, h1_ref, sh_ref, mod_ref, ln_ref, o_ref, buf, sem):
    tt = h1_ref.shape[0]

    def row_copy(t, k):
        return pltpu.make_async_copy(ys_ref.at[pl.ds(dest_ref[k, t], 1)], buf.at[k, pl.ds(t, 1)], sem)

    _row_dma_loops(tt, row_copy)
    y = sh_ref[...]
    for k in range(TOP_K):
        y = y + buf[k] * w_ref[:, k:k + 1]
    gate2 = mod_ref[3:4, :]
    o_ref[...] = (_layer_norm(DEEPNORM_ALPHA * h1_ref[...] + (1.0 + gate2) * y) * ln_ref[0:1, :]
                  + ln_ref[1:2, :])


def _combine(ys, dest, w_t, h1, shared, mod3, ln2):
    t = h1.shape[0]
    tt = min(CMB_TT, t)
    row = lambda i: (i, 0)
    const = lambda i: (0, 0)
    return pl.pallas_call(
        _combine_kernel,
        grid=(t // tt,),
        in_specs=[pl.BlockSpec((8, tt), lambda i: (0, i), memory_space=pltpu.SMEM),
                  pl.BlockSpec(memory_space=pl.ANY),
                  pl.BlockSpec((tt, 8), row),
                  pl.BlockSpec((tt, D_MODEL), row),
                  pl.BlockSpec((tt, D_MODEL), row),
                  pl.BlockSpec((8, D_MODEL), const),
                  pl.BlockSpec((8, D_MODEL), const)],
        out_specs=pl.BlockSpec((tt, D_MODEL), row),
        out_shape=jax.ShapeDtypeStruct((t, D_MODEL), F32),
        scratch_shapes=[pltpu.VMEM((TOP_K, tt, D_MODEL), F32), pltpu.SemaphoreType.DMA(())],
        compiler_params=_cparams(("arbitrary",)),
        name="moe_combine",
    )(dest, ys, w_t, h1, shared, mod3, ln2)


def _pad_rows(rows, n=8):
    a = jnp.concatenate(rows, axis=0)
    return jnp.pad(a, ((0, n - a.shape[0]), (0, 0)))


def _mixer(x2, scale1, shift1, w_in, m_igate_bias, m_fgate_bias, g_conv_w, g_A_log, g_dt_bias):
    t = x2.shape[0]
    c0 = 2 * M_QK + 2 * M_V
    c1 = c0 + 4 * M_HEADS
    c2 = c1 + 4 * G_W
    w_main = jnp.concatenate([w_in[:, :c0], w_in[:, c1:c2]], axis=1).astype(BF16)
    w_gate = jnp.pad(jnp.concatenate([w_in[:, c0:c1], w_in[:, c2:]], axis=1),
                     ((0, 0), (0, GATE_PAD - N_GATE))).astype(BF16)
    proj, graw = _inproj(x2, scale1, shift1, w_main, w_gate)
    bias_row = jnp.pad(jnp.concatenate([m_igate_bias.reshape(-1), m_fgate_bias.reshape(-1),
                                        jnp.zeros((2 * G_HEADS,), F32), g_dt_bias.reshape(-1)]),
                       (0, GATE_PAD - N_GATE)).reshape(1, GATE_PAD)
    alog_row = jnp.pad(g_A_log.reshape(-1), (COL_A, GATE_PAD - N_GATE)).reshape(1, GATE_PAD)
    g_col = _gateprep(graw, bias_row, alog_row)
    g_row = jnp.swapaxes(g_col.reshape(t // CHUNK, CHUNK, GATE_PAD), 1, 2)
    mf, mb = _mlstm(proj, g_col, g_row)
    qkv = _gdn_conv(proj, jnp.pad(g_conv_w, ((0, 8 - CONV_WIDTH), (0, 0))))
    gf, gb = _gdn(qkv, g_col, g_row)
    return proj, mf, mb, gf, gb


def kernel(x, c, w_ada, b_ada, w_in, m_igate_bias, m_fgate_bias, m_norm_w, g_conv_w, g_A_log, g_dt_bias,
           g_norm_w, w_out, ln1_w, ln1_b, router_w, router_bias, e_gate, e_up, e_down, s_gate, s_up,
           s_down, ln2_w, ln2_b):
    bsz, t, d = x.shape
    assert bsz == 1 and d == D_MODEL and w_ada.shape[0] == 1
    x2 = x[0]
    mod = _ada_mod(c, w_ada[0], b_ada[0])
    shift1, scale1, gate1, shift2, scale2, gate2 = [mod[:, i * d:(i + 1) * d] for i in range(6)]
    proj, mf, mb, gf, gb = _mixer(x2, scale1, shift1, w_in[0], m_igate_bias[0], m_fgate_bias[0],
                                  g_conv_w[0], g_A_log[0], g_dt_bias[0])
    mod3 = _pad_rows([gate1, scale2, shift2, gate2])
    ln1 = _pad_rows([ln1_w[0][None], ln1_b[0][None]])
    ln2 = _pad_rows([ln2_w[0][None], ln2_b[0][None]])
    h1, u2, logits_t = _outproj(
        mf, mb, gf, gb, proj, x2, mod3, m_norm_w[0][None], jnp.tile(g_norm_w[0], G_HEADS)[None],
        w_out[0].astype(BF16), ln1, router_w[0].T)
    top_idx, top_w, rank, counts = _route(logits_t, router_bias[0][:, None])
    counts = counts[:, 0]
    n_slots = (t * TOP_K + N_EXPERTS * (EXPERT_BLOCK - 1) + EXPERT_BLOCK - 1) // EXPERT_BLOCK * EXPERT_BLOCK
    n_blocks = n_slots // EXPERT_BLOCK
    padded = (counts + EXPERT_BLOCK - 1) // EXPERT_BLOCK * EXPERT_BLOCK
    padded_end = jnp.cumsum(padded)
    group_start = padded_end - padded
    expert_ids = jnp.arange(N_EXPERTS, dtype=jnp.int32)[:, None, None]
    dest = jnp.sum(jnp.where(top_idx[None] == expert_ids, group_start[:, None, None], 0), axis=0) + rank
    n_used = (padded_end[-1:] // EXPERT_BLOCK).astype(jnp.int32)
    blk_all = jnp.arange(n_blocks, dtype=jnp.int32) * EXPERT_BLOCK
    blk_row = jnp.minimum(blk_all, padded_end[-1] - EXPERT_BLOCK)
    block_e = jnp.sum((padded_end[None, :] <= blk_row[:, None]).astype(jnp.int32), axis=1)
    bvalid = jnp.clip(jnp.take(group_start + counts, block_e) - blk_row, 0, EXPERT_BLOCK).astype(jnp.int32)
    zero_flag = ((bvalid < EXPERT_BLOCK) | (blk_all >= padded_end[-1])).astype(jnp.int32)
    xs = _dispatch(u2, dest, zero_flag, n_slots)
    half = ((bvalid <= GMM_HALF) & (blk_all < padded_end[-1])).astype(jnp.int32)
    ys = _swiglu_grouped(xs, e_gate[0], e_up[0], e_down[0],
                         (block_e, n_used) + _run_tables(block_e, n_used) + (half,))
    nsb = t // EXPERT_BLOCK
    shared_be, shared_nb = jnp.zeros((nsb,), jnp.int32), jnp.full((1,), nsb, jnp.int32)
    shared = _swiglu_grouped(u2, s_gate, s_up, s_down,
                             (shared_be, shared_nb) + _run_tables(shared_be, shared_nb) + (shared_be,))
    out = _combine(ys, dest, top_w.T, h1, shared, mod3, ln2)
    return out[None]
```

```python
import functools

import jax
import jax.numpy as jnp
from jax import lax
from jax.experimental import pallas as pl
from jax.experimental.pallas import tpu as pltpu

F32 = jnp.float32
BF16 = jnp.bfloat16

D_MODEL = 2048
M_HEADS = 4
M_QK_DIM = 128
M_V_DIM = 256
G_HEADS = 8
G_HEAD_DIM = 128
CONV_WIDTH = 5
CHUNK = 64
GATE_SOFTCAP = 15.0
N_EXPERTS = 64
TOP_K = 6
N_GROUPS = 8
TOPK_GROUPS = 4
D_EXPERT = 1408
ROUTED_SCALE = 2.5
EXPERT_BLOCK = 256
NORM_EPS = 1e-6
DEEPNORM_ALPHA = 2.0 ** 0.25

M_QK = M_HEADS * M_QK_DIM
M_V = M_HEADS * M_V_DIM
G_W = G_HEADS * G_HEAD_DIM
N_MAIN = 2 * M_QK + 2 * M_V + 3 * G_W + G_W
N_GATE = 2 * M_HEADS + 2 * M_HEADS + 2 * G_HEADS + 2 * G_HEADS
GATE_PAD = 128
COL_I = 0
COL_F = 8
COL_B = 16
COL_A = 32
OFF_MQ, OFF_MK, OFF_MV, OFF_MO = 0, 512, 1024, 2048
OFF_GQKV, OFF_GZ = 3072, 6144

VMEM_LIMIT = 56 * 1024 * 1024


def _cparams(sem):
    return pltpu.CompilerParams(dimension_semantics=sem, vmem_limit_bytes=VMEM_LIMIT)


def _dot(a, b):
    return jnp.dot(a, b, preferred_element_type=F32)


def _dot_nt(a, b):
    return lax.dot_general(a, b, (((1,), (1,)), ((), ())), preferred_element_type=F32)


def _dot_tn(a, b):
    return lax.dot_general(a, b, (((0,), (0,)), ((), ())), preferred_element_type=F32)


def _layer_norm(x):
    mu = jnp.mean(x, axis=-1, keepdims=True)
    xc = x - mu
    var = jnp.mean(xc * xc, axis=-1, keepdims=True)
    return xc * lax.rsqrt(var + NORM_EPS)


def _sigmoid(x):
    return 1.0 / (1.0 + jnp.exp(-x))


def _silu(x):
    return x * _sigmoid(x)


def _softplus(x):
    return jnp.maximum(x, 0.0) + jnp.log1p(jnp.exp(-jnp.abs(x)))


ADA_TN = 1024
ADA_RC = 64


def _ada_kernel(c_ref, w_ref, b_ref, o_ref):
    def body(r, acc):
        rows = pl.ds(pl.multiple_of(r * ADA_RC, ADA_RC), ADA_RC)
        cond = _silu(c_ref[rows, :])
        blk = w_ref[rows, :] * cond
        return acc + jnp.sum(blk.reshape(ADA_RC // 8, 8, ADA_TN), axis=0)

    acc = lax.fori_loop(0, D_MODEL // ADA_RC, body, jnp.zeros((8, ADA_TN), F32))
    o_ref[...] = jnp.sum(acc, axis=0, keepdims=True) + b_ref[...]


def _ada_mod(c, w_ada, b_ada):
    n = w_ada.shape[1]
    return pl.pallas_call(
        _ada_kernel,
        grid=(n // ADA_TN,),
        in_specs=[
            pl.BlockSpec((D_MODEL, 1), lambda j: (0, 0)),
            pl.BlockSpec((D_MODEL, ADA_TN), lambda j: (0, j)),
            pl.BlockSpec((1, ADA_TN), lambda j: (0, j)),
        ],
        out_specs=pl.BlockSpec((1, ADA_TN), lambda j: (0, j)),
        out_shape=jax.ShapeDtypeStruct((1, n), F32),
        compiler_params=_cparams(("arbitrary",)),
        name="ada_mod",
    )(c.reshape(D_MODEL, 1), w_ada, b_ada.reshape(1, n))


INP_TM = 1024
INP_TN = 1024


def _inproj_kernel(x_ref, sc_ref, sh_ref, w_ref, wg_ref, o_ref, og_ref, u_scr):
    @pl.when(pl.program_id(1) == 0)
    def _():
        u = _layer_norm(x_ref[...]) * (1.0 + sc_ref[...]) + sh_ref[...]
        ub = u.astype(BF16)
        u_scr[...] = ub
        og_ref[...] = _dot(ub, wg_ref[...])

    o_ref[...] = _dot(u_scr[...], w_ref[...])


def _inproj(x, scale, shift, w_main, w_gate):
    t = x.shape[0]
    tm = min(INP_TM, t)
    return pl.pallas_call(
        _inproj_kernel,
        grid=(t // tm, N_MAIN // INP_TN),
        in_specs=[
            pl.BlockSpec((tm, D_MODEL), lambda i, j: (i, 0)),
            pl.BlockSpec((1, D_MODEL), lambda i, j: (0, 0)),
            pl.BlockSpec((1, D_MODEL), lambda i, j: (0, 0)),
            pl.BlockSpec((D_MODEL, INP_TN), lambda i, j: (0, j)),
            pl.BlockSpec((D_MODEL, GATE_PAD), lambda i, j: (0, 0)),
        ],
        out_specs=[
            pl.BlockSpec((tm, INP_TN), lambda i, j: (i, j)),
            pl.BlockSpec((tm, GATE_PAD), lambda i, j: (i, 0)),
        ],
        out_shape=[
            jax.ShapeDtypeStruct((t, N_MAIN), F32),
            jax.ShapeDtypeStruct((t, GATE_PAD), F32),
        ],
        scratch_shapes=[pltpu.VMEM((tm, D_MODEL), BF16)],
        compiler_params=_cparams(("arbitrary", "arbitrary")),
        name="ln_inproj",
    )(x, scale, shift, w_main, w_gate)


GP_TM = 512


def _split3(x):
    hi = x.astype(BF16)
    r1 = x - hi.astype(F32)
    mid = r1.astype(BF16)
    lo = (r1 - mid.astype(F32)).astype(BF16)
    return hi, mid, lo


def _tri_dot(tri, x):
    hi, mid, lo = _split3(x)
    return _dot(tri, hi) + _dot(tri, mid) + _dot(tri, lo)


def _gateprep_kernel(g_ref, bias_ref, alog_ref, o_ref):
    tm = g_ref.shape[0]
    lane = lax.broadcasted_iota(jnp.int32, (CHUNK, GATE_PAD), 1)
    ii = lax.broadcasted_iota(jnp.int32, (CHUNK, CHUNK), 0)
    jj = lax.broadcasted_iota(jnp.int32, (CHUNK, CHUNK), 1)
    tril = jnp.where(jj <= ii, 1.0, 0.0).astype(BF16)
    triu = jnp.where(jj >= ii, 1.0, 0.0).astype(BF16)
    neg_a = -jnp.exp(alog_ref[...])
    is_i = lane < COL_F
    is_f = (lane >= COL_F) & (lane < COL_B)
    is_b = (lane >= COL_B) & (lane < COL_A)
    fwd_cum = ((lane >= COL_F) & (lane < COL_F + M_HEADS)) | ((lane >= COL_A) & (lane < COL_A + G_HEADS))
    bwd_cum = ((lane >= COL_F + M_HEADS) & (lane < COL_B)) | ((lane >= COL_A + G_HEADS) & (lane < N_GATE))
    for c in range(tm // CHUNK):
        rows = slice(c * CHUNK, (c + 1) * CHUNK)
        x = g_ref[rows, :] + bias_ref[...]
        cap = GATE_SOFTCAP * jnp.tanh(x / GATE_SOFTCAP)
        log_f = jnp.minimum(cap, 0.0) - jnp.log1p(jnp.exp(-jnp.abs(cap)))
        beta = _sigmoid(x)
        decay = neg_a * _softplus(x)
        act = jnp.where(is_i, cap, jnp.where(is_f, log_f, jnp.where(is_b, beta, decay)))
        cum_f = _tri_dot(tril, act)
        cum_b = _tri_dot(triu, act)
        o_ref[rows, :] = jnp.where(fwd_cum, cum_f, jnp.where(bwd_cum, cum_b, act))


def _gateprep(graw, bias_row, alog_row):
    t = graw.shape[0]
    tm = min(GP_TM, t)
    return pl.pallas_call(
        _gateprep_kernel,
        grid=(t // tm,),
        in_specs=[
            pl.BlockSpec((tm, GATE_PAD), lambda i: (i, 0)),
            pl.BlockSpec((1, GATE_PAD), lambda i: (0, 0)),
            pl.BlockSpec((1, GATE_PAD), lambda i: (0, 0)),
        ],
        out_specs=pl.BlockSpec((tm, GATE_PAD), lambda i: (i, 0)),
        out_shape=jax.ShapeDtypeStruct((t, GATE_PAD), F32),
        compiler_params=_cparams(("arbitrary",)),
        name="gate_prep",
    )(graw, bias_row, alog_row)


CV_TM = 256
HALO = 8


def _conv_kernel(cur_ref, prev_ref, next_ref, cw_ref, o_ref):
    i = pl.program_id(0)
    n = pl.num_programs(0)
    tm = cur_ref.shape[0]
    pad = CONV_WIDTH // 2
    keep_prev = jnp.where(i > 0, 1.0, 0.0)
    keep_next = jnp.where(i < n - 1, 1.0, 0.0)
    for cb in range(3 * G_HEADS):
        cols = slice(cb * G_HEAD_DIM, (cb + 1) * G_HEAD_DIM)
        xp = jnp.concatenate(
            [prev_ref[:, cols] * keep_prev, cur_ref[:, cols], next_ref[:, cols] * keep_next], axis=0)
        acc = jnp.zeros((tm, G_HEAD_DIM), F32)
        for w in range(CONV_WIDTH):
            lo = HALO - pad + w
            acc = acc + xp[lo:lo + tm, :] * cw_ref[w:w + 1, cols]
        y = _silu(acc)
        if cb < 2 * G_HEADS:
            y = y * lax.rsqrt(jnp.sum(y * y, axis=-1, keepdims=True) + NORM_EPS)
            if cb < G_HEADS:
                y = y * (G_HEAD_DIM ** -0.5)
        o_ref[:, cols] = y


def _gdn_conv(proj, conv_w):
    t = proj.shape[0]
    tm = min(CV_TM, t)
    nh = tm // HALO
    nb = t // HALO
    c3 = 3 * G_W
    cblk = OFF_GQKV // c3
    return pl.pallas_call(
        _conv_kernel,
        grid=(t // tm,),
        in_specs=[
            pl.BlockSpec((tm, c3), lambda i: (i, cblk)),
            pl.BlockSpec((HALO, c3), lambda i: (jnp.maximum(i * nh - 1, 0), cblk)),
            pl.BlockSpec((HALO, c3), lambda i: (jnp.minimum((i + 1) * nh, nb - 1), cblk)),
            pl.BlockSpec((8, c3), lambda i: (0, 0)),
        ],
        out_specs=pl.BlockSpec((tm, c3), lambda i: (i, 0)),
        out_shape=jax.ShapeDtypeStruct((t, c3), F32),
        compiler_params=_cparams(("arbitrary",)),
        name="gdn_conv",
    )(proj, proj, proj, conv_w)


SCAN_RB = 512


def _tri_masks(reverse):
    ii = lax.broadcasted_iota(jnp.int32, (CHUNK, CHUNK), 0)
    jj = lax.broadcasted_iota(jnp.int32, (CHUNK, CHUNK), 1)
    if reverse:
        return jj >= ii, jj > ii
    return jj <= ii, jj < ii


def _mlstm_chunks(chains):
    for ch in chains:
        incl, _ = _tri_masks(ch["reverse"])
        ch["d"] = jnp.where(incl, ch["bc_c"] - ch["bc_r"] + ch["ic_r"], -jnp.inf)
        ch["inter"] = ch["bc_c"] + ch["m"]
        ch["qb"] = ch["q"].astype(BF16)
        ch["vb"] = ch["v"].astype(BF16)
    for ch in chains:
        ch["qk"] = _dot_nt(ch["qb"], ch["k"].astype(BF16))
    for ch in chains:
        ch["rmax"] = jnp.max(ch["d"], axis=1, keepdims=True)
    for ch in chains:
        ch["qc"] = _dot(ch["qb"], ch["c"].astype(BF16))
    for ch in chains:
        ch["qn"] = jnp.sum(ch["q"] * ch["n"], axis=1, keepdims=True)
    for ch in chains:
        bc_c = ch["bc_c"]
        b_last = bc_c[0:1, :] if ch["reverse"] else bc_c[CHUNK - 1:CHUNK, :]
        w_log = b_last - bc_c + ch["ic_c"]
        ch["m_new"] = jnp.maximum(b_last + ch["m"], jnp.max(w_log, axis=0, keepdims=True))
        ch["carry"] = jnp.exp(b_last + ch["m"] - ch["m_new"])
        ch["kw"] = ch["k"] * jnp.exp(w_log - ch["m_new"])
    for ch in chains:
        ch["kv"] = _dot_tn(ch["kw"].astype(BF16), ch["vb"])
    for ch in chains:
        ch["m_row"] = jnp.maximum(ch["rmax"], ch["inter"])
        ch["s_inter"] = jnp.exp(ch["inter"] - ch["m_row"])
        ch["w"] = jnp.exp(ch["d"] - ch["m_row"]) * ch["qk"]
    for ch in chains:
        ch["wv"] = _dot(ch["w"].astype(BF16), ch["vb"])
    for ch in chains:
        ch["wsum"] = jnp.sum(ch["w"], axis=1, keepdims=True)
    for ch in chains:
        ch["c_new"] = ch["carry"] * ch["c"] + ch["kv"]
        ch["n_new"] = ch["carry"] * ch["n"] + jnp.sum(ch["kw"], axis=0, keepdims=True)
    for ch in chains:
        num = ch["wv"] + ch["s_inter"] * ch["qc"]
        den = ch["wsum"] + ch["s_inter"] * ch["qn"]
        ch["out"] = num / jnp.maximum(jnp.abs(den), jnp.exp(-ch["m_row"]))


def _mlstm_kernel(qf_ref, kf_ref, vf_ref, gcf_ref, grf_ref,
                  qb_ref, kb_ref, vb_ref, gcb_ref, grb_ref,
                  of_ref, ob_ref, c_scr, n_scr, m_scr):
    @pl.when(pl.program_id(0) == 0)
    def _():
        c_scr[...] = jnp.zeros_like(c_scr)
        n_scr[...] = jnp.zeros_like(n_scr)
        m_scr[...] = jnp.zeros_like(m_scr)

    ncb = qf_ref.shape[0] // CHUNK
    kscale = M_QK_DIM ** -0.5

    def body(c, carry):
        chains = []
        for reverse in (False, True):
            cc = (ncb - 1 - c) if reverse else c
            rows = pl.ds(pl.multiple_of(cc * CHUNK, CHUNK), CHUNK)
            q_ref, k_ref, v_ref, gc_ref, gr_ref, o_ref = (
                (qb_ref, kb_ref, vb_ref, gcb_ref, grb_ref, ob_ref) if reverse
                else (qf_ref, kf_ref, vf_ref, gcf_ref, grf_ref, of_ref))
            gc = gc_ref[rows, :]
            gr = gr_ref[cc]
            for h in range(M_HEADS):
                s = (1 if reverse else 0) * M_HEADS + h
                ci, cf = COL_I + s, COL_F + s
                chains.append(dict(
                    reverse=reverse, s=s, o_ref=o_ref, rows=rows, h=h,
                    q=q_ref[rows, h * M_QK_DIM:(h + 1) * M_QK_DIM],
                    k=k_ref[rows, h * M_QK_DIM:(h + 1) * M_QK_DIM] * kscale,
                    v=v_ref[rows, h * M_V_DIM:(h + 1) * M_V_DIM],
                    ic_c=gc[:, ci:ci + 1], bc_c=gc[:, cf:cf + 1],
                    ic_r=gr[ci:ci + 1, :], bc_r=gr[cf:cf + 1, :],
                    c=c_scr[s], n=n_scr[s], m=m_scr[s][:, 0:1]))
        _mlstm_chunks(chains)
        for ch in chains:
            h, s = ch["h"], ch["s"]
            ch["o_ref"][ch["rows"], h * M_V_DIM:(h + 1) * M_V_DIM] = ch["out"]
            c_scr[s] = ch["c_new"]
            n_scr[s] = ch["n_new"]
            m_scr[s] = jnp.broadcast_to(ch["m_new"], (1, 128))
        return carry

    lax.fori_loop(0, ncb, body, 0)


def _mlstm(proj, g_col, g_row):
    t = proj.shape[0]
    rb = min(SCAN_RB, t)
    nb = t // rb
    ncb = rb // CHUNK
    fwd = lambda s: s
    bwd = lambda s: nb - 1 - s

    def specs(rmap):
        return [
            pl.BlockSpec((rb, M_QK), lambda s: (rmap(s), OFF_MQ // M_QK)),
            pl.BlockSpec((rb, M_QK), lambda s: (rmap(s), OFF_MK // M_QK)),
            pl.BlockSpec((rb, M_V), lambda s: (rmap(s), OFF_MV // M_V)),
            pl.BlockSpec((rb, GATE_PAD), lambda s: (rmap(s), 0)),
            pl.BlockSpec((ncb, GATE_PAD, CHUNK), lambda s: (rmap(s), 0, 0)),
        ]

    ns = 2 * M_HEADS
    return pl.pallas_call(
        _mlstm_kernel,
        grid=(nb,),
        in_specs=specs(fwd) + specs(bwd),
        out_specs=[pl.BlockSpec((rb, M_V), lambda s: (s, 0)),
                   pl.BlockSpec((rb, M_V), lambda s: (nb - 1 - s, 0))],
        out_shape=[jax.ShapeDtypeStruct((t, M_V), F32)] * 2,
        scratch_shapes=[pltpu.VMEM((ns, M_QK_DIM, M_V_DIM), F32),
                        pltpu.VMEM((ns, 1, M_QK_DIM), F32),
                        pltpu.VMEM((ns, 1, 128), F32)],
        compiler_params=_cparams(("arbitrary",)),
        name="mlstm_scan",
    )(proj, proj, proj, g_col, g_row, proj, proj, proj, g_col, g_row)


GDN_CHUNKS_PER_ITER = 2


def _gdn_local(chains):
    ii = lax.broadcasted_iota(jnp.int32, (CHUNK, CHUNK), 0)
    jj = lax.broadcasted_iota(jnp.int32, (CHUNK, CHUNK), 1)
    eye = jnp.where(ii == jj, 1.0, 0.0)
    for ch in chains:
        incl, strict = _tri_masks(ch["reverse"])
        gc_c = ch["gc_c"]
        ch["strict"] = strict
        ch["decay"] = jnp.exp(jnp.where(incl, gc_c - ch["gc_r"], -jnp.inf))
        ch["kb16"] = ch["k"].astype(BF16)
        ch["kbeta"] = ch["k"] * ch["beta_c"]
        g_last = gc_c[0:1, :] if ch["reverse"] else gc_c[CHUNK - 1:CHUNK, :]
        ch["s_decay"] = jnp.exp(g_last)
        ch["kdec"] = (ch["k"] * jnp.exp(g_last - gc_c)).astype(BF16)
        ch["qg"] = (ch["q"] * jnp.exp(gc_c)).astype(BF16)
    for ch in chains:
        ch["kk"] = _dot_nt(ch["kbeta"].astype(BF16), ch["kb16"])
    for ch in chains:
        ch["attn"] = (_dot_nt(ch["q"].astype(BF16), ch["kb16"]) * ch["decay"]).astype(BF16)
    for ch in chains:
        ch["pw"] = -jnp.where(ch["strict"], ch["kk"] * ch["decay"], 0.0)
        ch["inv"] = eye + ch["pw"]
    for _ in range(5):
        for ch in chains:
            pwb = ch["pw"].astype(BF16)
            ch["pw"] = _dot(pwb, pwb)
        for ch in chains:
            ch["inv"] = ch["inv"] + _dot(ch["inv"].astype(BF16), ch["pw"].astype(BF16))
    for ch in chains:
        rhs = jnp.concatenate([ch["v"] * ch["beta_c"], ch["kbeta"] * jnp.exp(ch["gc_c"])], axis=1)
        sol = _dot(ch["inv"].astype(BF16), rhs.astype(BF16))
        ch["u"] = sol[:, :G_HEAD_DIM]
        ch["w"] = sol[:, G_HEAD_DIM:].astype(BF16)


def _gdn_state(chains):
    for ch in chains:
        ch["sb"] = ch["s"].astype(BF16)
    for ch in chains:
        ch["vnb"] = (ch["u"] - _dot(ch["w"], ch["sb"])).astype(BF16)
    for ch in chains:
        ch["qs"] = _dot(ch["qg"], ch["sb"])
    for ch in chains:
        ch["s_new"] = ch["s"] * ch["s_decay"] + _dot_tn(ch["kdec"], ch["vnb"])
    for ch in chains:
        ch["out"] = ch["qs"] + _dot(ch["attn"], ch["vnb"])


def _gdn_kernel(qf_ref, kf_ref, vf_ref, gcf_ref, grf_ref,
                qb_ref, kb_ref, vb_ref, gcb_ref, grb_ref,
                of_ref, ob_ref, s_scr):
    @pl.when(pl.program_id(0) == 0)
    def _():
        s_scr[...] = jnp.zeros_like(s_scr)

    ncb = qf_ref.shape[0] // CHUNK

    def load_chains(c):
        chains = []
        for reverse in (False, True):
            cc = (ncb - 1 - c) if reverse else c
            rows = pl.ds(pl.multiple_of(cc * CHUNK, CHUNK), CHUNK)
            q_ref, k_ref, v_ref, gc_ref, gr_ref, o_ref = (
                (qb_ref, kb_ref, vb_ref, gcb_ref, grb_ref, ob_ref) if reverse
                else (qf_ref, kf_ref, vf_ref, gcf_ref, grf_ref, of_ref))
            gc = gc_ref[rows, :]
            gr = gr_ref[cc]
            for h in range(G_HEADS):
                s = (1 if reverse else 0) * G_HEADS + h
                cb, ca = COL_B + s, COL_A + s
                cols = slice(h * G_HEAD_DIM, (h + 1) * G_HEAD_DIM)
                chains.append(dict(
                    reverse=reverse, slot=s, o_ref=o_ref, rows=rows, cols=cols,
                    q=q_ref[rows, cols], k=k_ref[rows, cols], v=v_ref[rows, cols],
                    beta_c=gc[:, cb:cb + 1], gc_c=gc[:, ca:ca + 1], gc_r=gr[ca:ca + 1, :]))
        return chains

    def body(p, carry):
        groups = [load_chains(p * GDN_CHUNKS_PER_ITER + i) for i in range(GDN_CHUNKS_PER_ITER)]
        _gdn_local([ch for g in groups for ch in g])
        state = [s_scr[s] for s in range(2 * G_HEADS)]
        for g in groups:
            for ch in g:
                ch["s"] = state[ch["slot"]]
            _gdn_state(g)
            for ch in g:
                ch["o_ref"][ch["rows"], ch["cols"]] = ch["out"]
                state[ch["slot"]] = ch["s_new"]
        for s in range(2 * G_HEADS):
            s_scr[s] = state[s]
        return carry

    lax.fori_loop(0, ncb // GDN_CHUNKS_PER_ITER, body, 0)


def _gdn(qkv, g_col, g_row):
    t = qkv.shape[0]
    rb = min(SCAN_RB, t)
    nb = t // rb
    ncb = rb // CHUNK
    fwd = lambda s: s
    bwd = lambda s: nb - 1 - s

    def specs(rmap):
        return [
            pl.BlockSpec((rb, G_W), lambda s: (rmap(s), 0)),
            pl.BlockSpec((rb, G_W), lambda s: (rmap(s), 1)),
            pl.BlockSpec((rb, G_W), lambda s: (rmap(s), 2)),
            pl.BlockSpec((rb, GATE_PAD), lambda s: (rmap(s), 0)),
            pl.BlockSpec((ncb, GATE_PAD, CHUNK), lambda s: (rmap(s), 0, 0)),
        ]

    return pl.pallas_call(
        _gdn_kernel,
        grid=(nb,),
        in_specs=specs(fwd) + specs(bwd),
        out_specs=[pl.BlockSpec((rb, G_W), lambda s: (s, 0)),
                   pl.BlockSpec((rb, G_W), lambda s: (nb - 1 - s, 0))],
        out_shape=[jax.ShapeDtypeStruct((t, G_W), F32)] * 2,
        scratch_shapes=[pltpu.VMEM((2 * G_HEADS, G_HEAD_DIM, G_HEAD_DIM), F32)],
        compiler_params=_cparams(("arbitrary",)),
        name="gdn_scan",
    )(qkv, qkv, qkv, g_col, g_row, qkv, qkv, qkv, g_col, g_row)


OP_TM = 256
OP_SUB = 128


def _outproj_kernel(mf_ref, mb_ref, gf_ref, gb_ref, mo_ref, gz_ref, x_ref, mod_ref,
                    mnw_ref, gnw_ref, wout_ref, ln_ref, rw_ref,
                    h1_ref, u2_ref, lg_ref):
    gate1, scale2, shift2 = mod_ref[0:1, :], mod_ref[1:2, :], mod_ref[2:3, :]
    subs = [slice(r, r + OP_SUB) for r in range(0, x_ref.shape[0], OP_SUB)]
    mixed, y, h1s, u2s = {}, {}, {}, {}
    for i, rows in enumerate(subs):
        hm = mf_ref[rows, :] + mb_ref[rows, :]
        hg = gf_ref[rows, :] + gb_ref[rows, :]
        parts = []
        for h in range(M_HEADS):
            seg = hm[:, h * M_V_DIM:(h + 1) * M_V_DIM]
            parts.append(seg * lax.rsqrt(jnp.mean(seg * seg, axis=-1, keepdims=True) + NORM_EPS))
        hm_n = jnp.concatenate(parts, axis=1) * mnw_ref[...] * _sigmoid(mo_ref[rows, :])
        parts = []
        for h in range(G_HEADS):
            seg = hg[:, h * G_HEAD_DIM:(h + 1) * G_HEAD_DIM]
            parts.append(seg * lax.rsqrt(jnp.mean(seg * seg, axis=-1, keepdims=True) + NORM_EPS))
        hg_n = jnp.concatenate(parts, axis=1) * gnw_ref[...] * _silu(gz_ref[rows, :])
        mixed[i] = jnp.concatenate([hm_n, hg_n], axis=1).astype(BF16)
    for i, rows in enumerate(subs):
        y[i] = _dot(mixed[i], wout_ref[...])
    for i, rows in enumerate(subs):
        h1 = (_layer_norm(DEEPNORM_ALPHA * x_ref[rows, :] + (1.0 + gate1) * y[i]) * ln_ref[0:1, :]
              + ln_ref[1:2, :])
        h1_ref[rows, :] = h1
        h1s[i] = h1
    for i, rows in enumerate(subs):
        u2 = _layer_norm(h1s[i]) * (1.0 + scale2) + shift2
        u2_ref[rows, :] = u2
        u2s[i] = u2
    for i, rows in enumerate(subs):
        lg_ref[:, rows] = lax.dot_general(rw_ref[...], u2s[i], (((1,), (1,)), ((), ())),
                                          precision=lax.Precision.HIGHEST, preferred_element_type=F32)


def _outproj(mf, mb, gf, gb, proj, x, mod3, mnw, gnw, w_out, ln1, rw_t):
    t = x.shape[0]
    tm = min(OP_TM, t)
    row = lambda i: (i, 0)
    const = lambda i: (0, 0)
    return pl.pallas_call(
        _outproj_kernel,
        grid=(t // tm,),
        in_specs=[
            pl.BlockSpec((tm, M_V), row), pl.BlockSpec((tm, M_V), row),
            pl.BlockSpec((tm, G_W), row), pl.BlockSpec((tm, G_W), row),
            pl.BlockSpec((tm, M_V), lambda i: (i, OFF_MO // M_V)),
            pl.BlockSpec((tm, G_W), lambda i: (i, OFF_GZ // G_W)),
            pl.BlockSpec((tm, D_MODEL), row),
            pl.BlockSpec((8, D_MODEL), const),
            pl.BlockSpec((1, M_V), const), pl.BlockSpec((1, G_W), const),
            pl.BlockSpec((D_MODEL, D_MODEL), const),
            pl.BlockSpec((8, D_MODEL), const),
            pl.BlockSpec((N_EXPERTS, D_MODEL), const),
        ],
        out_specs=[
            pl.BlockSpec((tm, D_MODEL), row),
            pl.BlockSpec((tm, D_MODEL), row),
            pl.BlockSpec((N_EXPERTS, tm), lambda i: (0, i)),
        ],
        out_shape=[
            jax.ShapeDtypeStruct((t, D_MODEL), F32),
            jax.ShapeDtypeStruct((t, D_MODEL), F32),
            jax.ShapeDtypeStruct((N_EXPERTS, t), F32),
        ],
        compiler_params=_cparams(("arbitrary",)),
        name="outproj_ln",
    )(mf, mb, gf, gb, proj, proj, x, mod3, mnw, gnw, w_out, ln1, rw_t)


RT_TN = 512
GROUP_SIZE = N_EXPERTS // N_GROUPS


def _route_kernel(lg_ref, bias_ref, idx_ref, w_ref, rank_ref, cnt_ref, carry_scr):
    tn = lg_ref.shape[1]

    @pl.when(pl.program_id(0) == 0)
    def _():
        carry_scr[...] = jnp.zeros_like(carry_scr)

    neg = -jnp.inf
    scores = _sigmoid(lg_ref[...])
    biased = scores + bias_ref[...]
    sub8 = lax.broadcasted_iota(jnp.int32, (GROUP_SIZE, tn), 0).astype(F32)
    gscore = []
    for g in range(N_GROUPS):
        bg = biased[g * GROUP_SIZE:(g + 1) * GROUP_SIZE, :]
        m1 = jnp.max(bg, axis=0, keepdims=True)
        first = jnp.min(jnp.where(bg == m1, sub8, float(GROUP_SIZE)), axis=0, keepdims=True)
        m2 = jnp.max(jnp.where(sub8 == first, neg, bg), axis=0, keepdims=True)
        gscore.append(m1 + m2)
    masked = []
    for g in range(N_GROUPS):
        beaten = jnp.zeros((1, tn), F32)
        for g2 in range(N_GROUPS):
            if g2 == g:
                continue
            wins = (gscore[g2] >= gscore[g]) if g2 < g else (gscore[g2] > gscore[g])
            beaten = beaten + jnp.where(wins, 1.0, 0.0)
        keep = beaten < float(TOPK_GROUPS)
        masked.append(jnp.where(keep, biased[g * GROUP_SIZE:(g + 1) * GROUP_SIZE, :], neg))
    x = jnp.concatenate(masked, axis=0)
    eidx = lax.broadcasted_iota(jnp.int32, (N_EXPERTS, tn), 0).astype(F32)
    member = jnp.zeros((N_EXPERTS, tn), F32)
    sel_idx, sel_s = [], []
    for _ in range(TOP_K):
        m = jnp.max(x, axis=0, keepdims=True)
        idx = jnp.min(jnp.where(x == m, eidx, float(N_EXPERTS)), axis=0, keepdims=True)
        sel = eidx == idx
        sel_idx.append(idx)
        sel_s.append(jnp.sum(jnp.where(sel, scores, 0.0), axis=0, keepdims=True))
        member = member + jnp.where(sel, 1.0, 0.0)
        x = jnp.where(sel, neg, x)
    total = sel_s[0]
    for s in sel_s[1:]:
        total = total + s
    ti = lax.broadcasted_iota(jnp.int32, (tn, tn), 0)
    tj = lax.broadcasted_iota(jnp.int32, (tn, tn), 1)
    before = jnp.where(ti < tj, 1.0, 0.0).astype(BF16)
    carry = carry_scr[:, 0:1]
    excl = _dot(member.astype(BF16), before) + carry
    ranks = [jnp.sum(jnp.where(eidx == i, excl, 0.0), axis=0, keepdims=True) for i in sel_idx]
    zero = jnp.zeros((1, tn), F32)
    idx_ref[...] = jnp.concatenate(sel_idx + [zero, zero], axis=0).astype(jnp.int32)
    w_ref[...] = jnp.concatenate([ROUTED_SCALE * s / total for s in sel_s] + [zero, zero], axis=0)
    rank_ref[...] = jnp.concatenate(ranks + [zero, zero], axis=0).astype(jnp.int32)
    new_carry = carry + jnp.sum(member, axis=1, keepdims=True)
    carry_scr[...] = jnp.broadcast_to(new_carry, carry_scr.shape)
    cnt_ref[...] = jnp.broadcast_to(new_carry, cnt_ref.shape).astype(jnp.int32)


def _route(logits_t, bias_col):
    t = logits_t.shape[1]
    tn = min(RT_TN, t)
    col = lambda i: (0, i)
    return pl.pallas_call(
        _route_kernel,
        grid=(t // tn,),
        in_specs=[pl.BlockSpec((N_EXPERTS, tn), col), pl.BlockSpec((N_EXPERTS, 1), lambda i: (0, 0))],
        out_specs=[pl.BlockSpec((8, tn), col), pl.BlockSpec((8, tn), col), pl.BlockSpec((8, tn), col),
                   pl.BlockSpec((N_EXPERTS, 128), lambda i: (0, 0))],
        out_shape=[jax.ShapeDtypeStruct((8, t), jnp.int32), jax.ShapeDtypeStruct((8, t), F32),
                   jax.ShapeDtypeStruct((8, t), jnp.int32), jax.ShapeDtypeStruct((N_EXPERTS, 128), jnp.int32)],
        scratch_shapes=[pltpu.VMEM((N_EXPERTS, 128), F32)],
        compiler_params=_cparams(("arbitrary",)),
        name="route_topk",
    )(logits_t, bias_col)


GMM_PAIRS = 2
GMM_HALF = EXPERT_BLOCK // 2


def _gmm_kernel(be_ref, nb_ref, first_ref, slot_ref, nxt_ref, half_ref, x_ref, w_hbm, *rest, mode):
    wbuf, sem = rest[-2], rest[-1]
    o_ref = rest[-3]
    tb = EXPERT_BLOCK
    nb = nb_ref[0]

    def weight_copy(e, s):
        return pltpu.make_async_copy(w_hbm.at[e], wbuf.at[s], sem.at[s])

    def enter_run(b):
        @pl.when(first_ref[b] == 1)
        def _():
            weight_copy(be_ref[b], slot_ref[b]).wait()

            @pl.when(nxt_ref[b] >= 0)
            def _():
                weight_copy(nxt_ref[b], 1 - slot_ref[b]).start()

    def zero(r0, nrows):
        o_ref[pl.ds(r0, nrows), :] = jnp.zeros((nrows, o_ref.shape[1]), o_ref.dtype)

    def compute(r0, nrows, b):
        rows = pl.ds(r0, nrows)
        acc = _dot(x_ref[rows, :].astype(BF16), wbuf[slot_ref[b]].astype(BF16))
        if mode == "silu":
            acc = _silu(acc)
        elif mode == "mul":
            acc = acc * rest[0][rows, :].astype(F32)
        o_ref[rows, :] = acc.astype(o_ref.dtype)

    def trimmed(r0, nrows, b_run, b_last):
        @pl.when(half_ref[b_last] == 0)
        def _():
            compute(r0, nrows, b_run)

        @pl.when(half_ref[b_last] == 1)
        def _():
            compute(r0, nrows - GMM_HALF, b_run)
            zero(r0 + nrows - GMM_HALF, GMM_HALF)

    def block(r0, b):
        @pl.when(b < nb)
        def _():
            trimmed(r0, tb, b, b)

        @pl.when(b >= nb)
        def _():
            zero(r0, tb)

    @pl.when(pl.program_id(0) == 0)
    def _():
        weight_copy(be_ref[0], 0).start()

    def pair_body(pair, carry):
        b0 = (pl.program_id(0) * GMM_PAIRS + pair) * 2
        b1 = b0 + 1
        r0 = pl.multiple_of(pair * 2 * tb, 2 * tb)
        enter_run(b0)
        same_run = (b1 < nb) & (first_ref[b1] == 0)

        @pl.when(same_run)
        def _():
            trimmed(r0, 2 * tb, b0, b1)

        @pl.when(jnp.logical_not(same_run))
        def _():
            block(r0, b0)
            enter_run(b1)
            block(r0 + tb, b1)

        return carry

    lax.fori_loop(0, GMM_PAIRS, pair_body, 0)


def _run_tables(block_e, n_used):
    nb = block_e.shape[0]
    idx = jnp.arange(nb, dtype=jnp.int32)
    used = idx < n_used[0]
    prev = jnp.concatenate([jnp.full((1,), -1, jnp.int32), block_e[:-1]])
    first = ((block_e != prev) & used).astype(jnp.int32)
    slot = (jnp.cumsum(first) - 1) & 1
    later_first = jnp.where(first == 1, idx, nb)
    nxt_pos = jnp.flip(lax.cummin(jnp.flip(jnp.concatenate([later_first[1:], jnp.full((1,), nb, jnp.int32)]))))
    nxt = jnp.where(nxt_pos < nb, jnp.take(block_e, jnp.minimum(nxt_pos, nb - 1)), -1)
    return first, slot.astype(jnp.int32), nxt.astype(jnp.int32)


def _gmm(x, w, tables, extra, mode, out_dtype):
    block_e, n_used, first, slot, nxt, half = tables
    rows, k = x.shape
    n = w.shape[2]
    bps = 2 * GMM_PAIRS
    tb = bps * EXPERT_BLOCK
    xmap = lambda p, be, nb, fi, sl, nx, hf: (jnp.minimum(p, (nb[0] - 1) // bps), 0)
    in_specs = [pl.BlockSpec((tb, k), xmap), pl.BlockSpec(memory_space=pl.ANY)]
    args = [x, w]
    if mode == "mul":
        in_specs.append(pl.BlockSpec((tb, n), xmap))
        args.append(extra)
    return pl.pallas_call(
        functools.partial(_gmm_kernel, mode=mode),
        grid_spec=pltpu.PrefetchScalarGridSpec(
            num_scalar_prefetch=6,
            grid=(rows // tb,),
            in_specs=in_specs,
            out_specs=pl.BlockSpec((tb, n), lambda p, be, nb, fi, sl, nx, hf: (p, 0)),
            scratch_shapes=[pltpu.VMEM((2, k, n), F32), pltpu.SemaphoreType.DMA((2,))],
        ),
        out_shape=jax.ShapeDtypeStruct((rows, n), out_dtype),
        compiler_params=_cparams(("arbitrary",)),
        name="gmm_" + mode,
    )(block_e, n_used, first, slot, nxt, half, *args)


def _swiglu_grouped(x, wg, wu, wd, tables):
    a = _gmm(x, wg, tables, None, "silu", BF16)
    h = _gmm(x, wu, tables, a, "mul", BF16)
    return _gmm(h, wd, tables, None, "plain", F32)


DSP_TT = 512
ROW_UNROLL = 8


def _dispatch_kernel(zf_ref, dest_ref, u_ref, xs_ref, zero_scr, sem, zsem):
    tt = u_ref.shape[0]

    @pl.when(pl.program_id(0) == 0)
    def _():
        zero_scr[...] = jnp.zeros_like(zero_scr)

        def block_fill(b):
            rows = pl.ds(pl.multiple_of(b * EXPERT_BLOCK, EXPERT_BLOCK), EXPERT_BLOCK)
            return pltpu.make_async_copy(zero_scr, xs_ref.at[rows], zsem)

        def fill_start(b, carry):
            @pl.when(zf_ref[b] != 0)
            def _():
                block_fill(b).start()
            return carry

        def fill_wait(b, carry):
            @pl.when(zf_ref[b] != 0)
            def _():
                block_fill(b).wait()
            return carry

        lax.fori_loop(0, zf_ref.shape[0], fill_start, 0)
        lax.fori_loop(0, zf_ref.shape[0], fill_wait, 0)

    def row_copy(t, k):
        return pltpu.make_async_copy(u_ref.at[pl.ds(t, 1)], xs_ref.at[pl.ds(dest_ref[k, t], 1)], sem)

    _row_dma_loops(tt, row_copy)


def _row_dma_loops(tt, row_copy, start=True, wait=True):
    def issue(g, carry):
        t0 = pl.multiple_of(g * ROW_UNROLL, ROW_UNROLL)
        for r in range(ROW_UNROLL):
            for k in range(TOP_K):
                row_copy(t0 + r, k).start(priority=k % 2)
        return carry

    def drain(g, carry):
        t0 = pl.multiple_of(g * ROW_UNROLL, ROW_UNROLL)
        for r in range(ROW_UNROLL):
            for k in range(TOP_K):
                row_copy(t0 + r, k).wait()
        return carry

    if start:
        lax.fori_loop(0, tt // ROW_UNROLL, issue, 0)
    if wait:
        lax.fori_loop(0, tt // ROW_UNROLL, drain, 0)


def _dispatch(u2, dest, zero_flag, n_slots):
    t = u2.shape[0]
    tt = min(DSP_TT, t)
    return pl.pallas_call(
        _dispatch_kernel,
        grid_spec=pltpu.PrefetchScalarGridSpec(
            num_scalar_prefetch=1,
            grid=(t // tt,),
            in_specs=[pl.BlockSpec((8, tt), lambda i, zf: (0, i), memory_space=pltpu.SMEM),
                      pl.BlockSpec((tt, D_MODEL), lambda i, zf: (i, 0))],
            out_specs=pl.BlockSpec(memory_space=pl.ANY),
            scratch_shapes=[pltpu.VMEM((EXPERT_BLOCK, D_MODEL), F32),
                            pltpu.SemaphoreType.DMA(()), pltpu.SemaphoreType.DMA(())],
        ),
        out_shape=jax.ShapeDtypeStruct((n_slots, D_MODEL), F32),
        compiler_params=_cparams(("arbitrary",)),
        name="moe_dispatch",
    )(zero_flag, dest, u2)


CMB_TT = 128


def _combine_kernel(dest_ref, dnext_ref, ys_ref, w_ref, h1_ref, sh_ref, mod_ref, ln_ref, o_ref, buf, sem):
    tt = h1_ref.shape[0]
    i = pl.program_id(0)
    slot = i % 2

    def row_copy(idx_ref, s):
        def make(t, k):
            return pltpu.make_async_copy(ys_ref.at[pl.ds(idx_ref[k, t], 1)],
                                         buf.at[s, k, pl.ds(t, 1)], sem.at[s])
        return make

    @pl.when(i == 0)
    def _():
        _row_dma_loops(tt, row_copy(dest_ref, slot), wait=False)

    @pl.when(i + 1 < pl.num_programs(0))
    def _():
        _row_dma_loops(tt, row_copy(dnext_ref, 1 - slot), wait=False)

    _row_dma_loops(tt, row_copy(dest_ref, slot), start=False)
    y = sh_ref[...]
    for k in range(TOP_K):
        y = y + buf[slot, k] * w_ref[:, k:k + 1]
    gate2 = mod_ref[3:4, :]
    o_ref[...] = (_layer_norm(DEEPNORM_ALPHA * h1_ref[...] + (1.0 + gate2) * y) * ln_ref[0:1, :]
                  + ln_ref[1:2, :])


def _combine(ys, dest, w_t, h1, shared, mod3, ln2):
    t = h1.shape[0]
    tt = min(CMB_TT, t)
    row = lambda i: (i, 0)
    const = lambda i: (0, 0)
    last = t // tt - 1
    return pl.pallas_call(
        _combine_kernel,
        grid=(t // tt,),
        in_specs=[pl.BlockSpec((8, tt), lambda i: (0, i), memory_space=pltpu.SMEM),
                  pl.BlockSpec((8, tt), lambda i: (0, jnp.minimum(i + 1, last)), memory_space=pltpu.SMEM),
                  pl.BlockSpec(memory_space=pl.ANY),
                  pl.BlockSpec((tt, 8), row),
                  pl.BlockSpec((tt, D_MODEL), row),
                  pl.BlockSpec((tt, D_MODEL), row),
                  pl.BlockSpec((8, D_MODEL), const),
                  pl.BlockSpec((8, D_MODEL), const)],
        out_specs=pl.BlockSpec((tt, D_MODEL), row),
        out_shape=jax.ShapeDtypeStruct((t, D_MODEL), F32),
        scratch_shapes=[pltpu.VMEM((2, TOP_K, tt, D_MODEL), F32), pltpu.SemaphoreType.DMA((2,))],
        compiler_params=_cparams(("arbitrary",)),
        name="moe_combine",
    )(dest, dest, ys, w_t, h1, shared, mod3, ln2)


def _pad_rows(rows, n=8):
    a = jnp.concatenate(rows, axis=0)
    return jnp.pad(a, ((0, n - a.shape[0]), (0, 0)))


def _mixer(x2, scale1, shift1, w_in, m_igate_bias, m_fgate_bias, g_conv_w, g_A_log, g_dt_bias):
    t = x2.shape[0]
    c0 = 2 * M_QK + 2 * M_V
    c1 = c0 + 4 * M_HEADS
    c2 = c1 + 4 * G_W
    w_main = jnp.concatenate([w_in[:, :c0], w_in[:, c1:c2]], axis=1).astype(BF16)
    w_gate = jnp.pad(jnp.concatenate([w_in[:, c0:c1], w_in[:, c2:]], axis=1),
                     ((0, 0), (0, GATE_PAD - N_GATE))).astype(BF16)
    proj, graw = _inproj(x2, scale1, shift1, w_main, w_gate)
    bias_row = jnp.pad(jnp.concatenate([m_igate_bias.reshape(-1), m_fgate_bias.reshape(-1),
                                        jnp.zeros((2 * G_HEADS,), F32), g_dt_bias.reshape(-1)]),
                       (0, GATE_PAD - N_GATE)).reshape(1, GATE_PAD)
    alog_row = jnp.pad(g_A_log.reshape(-1), (COL_A, GATE_PAD - N_GATE)).reshape(1, GATE_PAD)
    g_col = _gateprep(graw, bias_row, alog_row)
    g_row = jnp.swapaxes(g_col.reshape(t // CHUNK, CHUNK, GATE_PAD), 1, 2)
    mf, mb = _mlstm(proj, g_col, g_row)
    qkv = _gdn_conv(proj, jnp.pad(g_conv_w, ((0, 8 - CONV_WIDTH), (0, 0))))
    gf, gb = _gdn(qkv, g_col, g_row)
    return proj, mf, mb, gf, gb


def kernel(x, c, w_ada, b_ada, w_in, m_igate_bias, m_fgate_bias, m_norm_w, g_conv_w, g_A_log, g_dt_bias,
           g_norm_w, w_out, ln1_w, ln1_b, router_w, router_bias, e_gate, e_up, e_down, s_gate, s_up,
           s_down, ln2_w, ln2_b):
    bsz, t, d = x.shape
    assert bsz == 1 and d == D_MODEL and w_ada.shape[0] == 1
    x2 = x[0]
    mod = _ada_mod(c, w_ada[0], b_ada[0])
    shift1, scale1, gate1, shift2, scale2, gate2 = [mod[:, i * d:(i + 1) * d] for i in range(6)]
    proj, mf, mb, gf, gb = _mixer(x2, scale1, shift1, w_in[0], m_igate_bias[0], m_fgate_bias[0],
                                  g_conv_w[0], g_A_log[0], g_dt_bias[0])
    mod3 = _pad_rows([gate1, scale2, shift2, gate2])
    ln1 = _pad_rows([ln1_w[0][None], ln1_b[0][None]])
    ln2 = _pad_rows([ln2_w[0][None], ln2_b[0][None]])
    h1, u2, logits_t = _outproj(
        mf, mb, gf, gb, proj, x2, mod3, m_norm_w[0][None], jnp.tile(g_norm_w[0], G_HEADS)[None],
        w_out[0].astype(BF16), ln1, router_w[0].T)
    top_idx, top_w, rank, counts = _route(logits_t, router_bias[0][:, None])
    counts = counts[:, 0]
    n_slots = (t * TOP_K + N_EXPERTS * (EXPERT_BLOCK - 1) + EXPERT_BLOCK - 1) // EXPERT_BLOCK * EXPERT_BLOCK
    n_blocks = n_slots // EXPERT_BLOCK
    padded = (counts + EXPERT_BLOCK - 1) // EXPERT_BLOCK * EXPERT_BLOCK
    padded_end = jnp.cumsum(padded)
    group_start = padded_end - padded
    expert_ids = jnp.arange(N_EXPERTS, dtype=jnp.int32)[:, None, None]
    dest = jnp.sum(jnp.where(top_idx[None] == expert_ids, group_start[:, None, None], 0), axis=0) + rank
    n_used = (padded_end[-1:] // EXPERT_BLOCK).astype(jnp.int32)
    blk_all = jnp.arange(n_blocks, dtype=jnp.int32) * EXPERT_BLOCK
    blk_row = jnp.minimum(blk_all, padded_end[-1] - EXPERT_BLOCK)
    block_e = jnp.sum((padded_end[None, :] <= blk_row[:, None]).astype(jnp.int32), axis=1)
    bvalid = jnp.clip(jnp.take(group_start + counts, block_e) - blk_row, 0, EXPERT_BLOCK).astype(jnp.int32)
    zero_flag = ((bvalid < EXPERT_BLOCK) | (blk_all >= padded_end[-1])).astype(jnp.int32)
    xs = _dispatch(u2, dest, zero_flag, n_slots)
    half = ((bvalid <= GMM_HALF) & (blk_all < padded_end[-1])).astype(jnp.int32)
    ys = _swiglu_grouped(xs, e_gate[0], e_up[0], e_down[0],
                         (block_e, n_used) + _run_tables(block_e, n_used) + (half,))
    nsb = t // EXPERT_BLOCK
    shared_be, shared_nb = jnp.zeros((nsb,), jnp.int32), jnp.full((1,), nsb, jnp.int32)
    shared = _swiglu_grouped(u2, s_gate, s_up, s_down,
                             (shared_be, shared_nb) + _run_tables(shared_be, shared_nb) + (shared_be,))
    out = _combine(ys, dest, top_w.T, h1, shared, mod3, ln2)
    return out[None]
```

```python
import functools

import jax
import jax.numpy as jnp
from jax import lax
from jax.experimental import pallas as pl
from jax.experimental.pallas import tpu as pltpu

F32 = jnp.float32
BF16 = jnp.bfloat16

D_MODEL = 2048
M_HEADS = 4
M_QK_DIM = 128
M_V_DIM = 256
G_HEADS = 8
G_HEAD_DIM = 128
CONV_WIDTH = 5
CHUNK = 64
GATE_SOFTCAP = 15.0
N_EXPERTS = 64
TOP_K = 6
N_GROUPS = 8
TOPK_GROUPS = 4
D_EXPERT = 1408
ROUTED_SCALE = 2.5
EXPERT_BLOCK = 256
NORM_EPS = 1e-6
DEEPNORM_ALPHA = 2.0 ** 0.25

M_QK = M_HEADS * M_QK_DIM
M_V = M_HEADS * M_V_DIM
G_W = G_HEADS * G_HEAD_DIM
N_MAIN = 2 * M_QK + 2 * M_V + 3 * G_W + G_W
N_GATE = 2 * M_HEADS + 2 * M_HEADS + 2 * G_HEADS + 2 * G_HEADS
GATE_PAD = 128
COL_I = 0
COL_F = 8
COL_B = 16
COL_A = 32
OFF_MQ, OFF_MK, OFF_MV, OFF_MO = 0, 512, 1024, 2048
OFF_GQKV, OFF_GZ = 3072, 6144

VMEM_LIMIT = 56 * 1024 * 1024


def _cparams(sem):
    return pltpu.CompilerParams(dimension_semantics=sem, vmem_limit_bytes=VMEM_LIMIT)


def _dot(a, b):
    return jnp.dot(a, b, preferred_element_type=F32)


def _dot_nt(a, b):
    return lax.dot_general(a, b, (((1,), (1,)), ((), ())), preferred_element_type=F32)


def _dot_tn(a, b):
    return lax.dot_general(a, b, (((0,), (0,)), ((), ())), preferred_element_type=F32)


def _layer_norm(x):
    mu = jnp.mean(x, axis=-1, keepdims=True)
    xc = x - mu
    var = jnp.mean(xc * xc, axis=-1, keepdims=True)
    return xc * lax.rsqrt(var + NORM_EPS)


def _sigmoid(x):
    return 1.0 / (1.0 + jnp.exp(-x))


def _silu(x):
    return x * _sigmoid(x)


def _softplus(x):
    return jnp.maximum(x, 0.0) + jnp.log1p(jnp.exp(-jnp.abs(x)))


ADA_TN = 1024
ADA_RC = 64


def _ada_kernel(c_ref, w_ref, b_ref, o_ref):
    def body(r, acc):
        rows = pl.ds(pl.multiple_of(r * ADA_RC, ADA_RC), ADA_RC)
        cond = _silu(c_ref[rows, :])
        blk = w_ref[rows, :] * cond
        return acc + jnp.sum(blk.reshape(ADA_RC // 8, 8, ADA_TN), axis=0)

    acc = lax.fori_loop(0, D_MODEL // ADA_RC, body, jnp.zeros((8, ADA_TN), F32), unroll=4)
    o_ref[...] = jnp.sum(acc, axis=0, keepdims=True) + b_ref[...]


def _ada_mod(c, w_ada, b_ada):
    n = w_ada.shape[1]
    return pl.pallas_call(
        _ada_kernel,
        grid=(n // ADA_TN,),
        in_specs=[
            pl.BlockSpec((D_MODEL, 1), lambda j: (0, 0)),
            pl.BlockSpec((D_MODEL, ADA_TN), lambda j: (0, j)),
            pl.BlockSpec((1, ADA_TN), lambda j: (0, j)),
        ],
        out_specs=pl.BlockSpec((1, ADA_TN), lambda j: (0, j)),
        out_shape=jax.ShapeDtypeStruct((1, n), F32),
        compiler_params=_cparams(("arbitrary",)),
        name="ada_mod",
    )(c.reshape(D_MODEL, 1), w_ada, b_ada.reshape(1, n))


INP_TM = 1024
INP_TN = 1024


def _inproj_kernel(x_ref, sc_ref, sh_ref, w_ref, wg_ref, o_ref, og_ref, u_scr):
    @pl.when(pl.program_id(1) == 0)
    def _():
        u = _layer_norm(x_ref[...]) * (1.0 + sc_ref[...]) + sh_ref[...]
        ub = u.astype(BF16)
        u_scr[...] = ub
        og_ref[...] = _dot(ub, wg_ref[...])

    o_ref[...] = _dot(u_scr[...], w_ref[...])


def _inproj(x, scale, shift, w_main, w_gate):
    t = x.shape[0]
    tm = min(INP_TM, t)
    return pl.pallas_call(
        _inproj_kernel,
        grid=(t // tm, N_MAIN // INP_TN),
        in_specs=[
            pl.BlockSpec((tm, D_MODEL), lambda i, j: (i, 0)),
            pl.BlockSpec((1, D_MODEL), lambda i, j: (0, 0)),
            pl.BlockSpec((1, D_MODEL), lambda i, j: (0, 0)),
            pl.BlockSpec((D_MODEL, INP_TN), lambda i, j: (0, j)),
            pl.BlockSpec((D_MODEL, GATE_PAD), lambda i, j: (0, 0)),
        ],
        out_specs=[
            pl.BlockSpec((tm, INP_TN), lambda i, j: (i, j)),
            pl.BlockSpec((tm, GATE_PAD), lambda i, j: (i, 0)),
        ],
        out_shape=[
            jax.ShapeDtypeStruct((t, N_MAIN), F32),
            jax.ShapeDtypeStruct((t, GATE_PAD), F32),
        ],
        scratch_shapes=[pltpu.VMEM((tm, D_MODEL), BF16)],
        compiler_params=_cparams(("arbitrary", "arbitrary")),
        name="ln_inproj",
    )(x, scale, shift, w_main, w_gate)


GP_TM = 512


def _split3(x):
    hi = x.astype(BF16)
    r1 = x - hi.astype(F32)
    mid = r1.astype(BF16)
    lo = (r1 - mid.astype(F32)).astype(BF16)
    return hi, mid, lo


def _tri_dot(tri, x):
    hi, mid, lo = _split3(x)
    return _dot(tri, hi) + _dot(tri, mid) + _dot(tri, lo)


def _gateprep_kernel(g_ref, bias_ref, alog_ref, o_ref):
    tm = g_ref.shape[0]
    lane = lax.broadcasted_iota(jnp.int32, (CHUNK, GATE_PAD), 1)
    ii = lax.broadcasted_iota(jnp.int32, (CHUNK, CHUNK), 0)
    jj = lax.broadcasted_iota(jnp.int32, (CHUNK, CHUNK), 1)
    tril = jnp.where(jj <= ii, 1.0, 0.0).astype(BF16)
    triu = jnp.where(jj >= ii, 1.0, 0.0).astype(BF16)
    neg_a = -jnp.exp(alog_ref[...])
    is_i = lane < COL_F
    is_f = (lane >= COL_F) & (lane < COL_B)
    is_b = (lane >= COL_B) & (lane < COL_A)
    fwd_cum = ((lane >= COL_F) & (lane < COL_F + M_HEADS)) | ((lane >= COL_A) & (lane < COL_A + G_HEADS))
    bwd_cum = ((lane >= COL_F + M_HEADS) & (lane < COL_B)) | ((lane >= COL_A + G_HEADS) & (lane < N_GATE))
    for c in range(tm // CHUNK):
        rows = slice(c * CHUNK, (c + 1) * CHUNK)
        x = g_ref[rows, :] + bias_ref[...]
        cap = GATE_SOFTCAP * jnp.tanh(x / GATE_SOFTCAP)
        log_f = jnp.minimum(cap, 0.0) - jnp.log1p(jnp.exp(-jnp.abs(cap)))
        beta = _sigmoid(x)
        decay = neg_a * _softplus(x)
        act = jnp.where(is_i, cap, jnp.where(is_f, log_f, jnp.where(is_b, beta, decay)))
        cum_f = _tri_dot(tril, act)
        cum_b = _tri_dot(triu, act)
        o_ref[rows, :] = jnp.where(fwd_cum, cum_f, jnp.where(bwd_cum, cum_b, act))


def _gateprep(graw, bias_row, alog_row):
    t = graw.shape[0]
    tm = min(GP_TM, t)
    return pl.pallas_call(
        _gateprep_kernel,
        grid=(t // tm,),
        in_specs=[
            pl.BlockSpec((tm, GATE_PAD), lambda i: (i, 0)),
            pl.BlockSpec((1, GATE_PAD), lambda i: (0, 0)),
            pl.BlockSpec((1, GATE_PAD), lambda i: (0, 0)),
        ],
        out_specs=pl.BlockSpec((tm, GATE_PAD), lambda i: (i, 0)),
        out_shape=jax.ShapeDtypeStruct((t, GATE_PAD), F32),
        compiler_params=_cparams(("arbitrary",)),
        name="gate_prep",
    )(graw, bias_row, alog_row)


CV_TM = 256
HALO = 8


def _conv_kernel(cur_ref, prev_ref, next_ref, cw_ref, o_ref):
    i = pl.program_id(0)
    n = pl.num_programs(0)
    tm = cur_ref.shape[0]
    pad = CONV_WIDTH // 2
    keep_prev = jnp.where(i > 0, 1.0, 0.0)
    keep_next = jnp.where(i < n - 1, 1.0, 0.0)
    for cb in range(3 * G_HEADS):
        cols = slice(cb * G_HEAD_DIM, (cb + 1) * G_HEAD_DIM)
        xp = jnp.concatenate(
            [prev_ref[:, cols] * keep_prev, cur_ref[:, cols], next_ref[:, cols] * keep_next], axis=0)
        acc = jnp.zeros((tm, G_HEAD_DIM), F32)
        for w in range(CONV_WIDTH):
            lo = HALO - pad + w
            acc = acc + xp[lo:lo + tm, :] * cw_ref[w:w + 1, cols]
        y = _silu(acc)
        if cb < 2 * G_HEADS:
            y = y * lax.rsqrt(jnp.sum(y * y, axis=-1, keepdims=True) + NORM_EPS)
            if cb < G_HEADS:
                y = y * (G_HEAD_DIM ** -0.5)
        o_ref[:, cols] = y


def _gdn_conv(proj, conv_w):
    t = proj.shape[0]
    tm = min(CV_TM, t)
    nh = tm // HALO
    nb = t // HALO
    c3 = 3 * G_W
    cblk = OFF_GQKV // c3
    return pl.pallas_call(
        _conv_kernel,
        grid=(t // tm,),
        in_specs=[
            pl.BlockSpec((tm, c3), lambda i: (i, cblk)),
            pl.BlockSpec((HALO, c3), lambda i: (jnp.maximum(i * nh - 1, 0), cblk)),
            pl.BlockSpec((HALO, c3), lambda i: (jnp.minimum((i + 1) * nh, nb - 1), cblk)),
            pl.BlockSpec((8, c3), lambda i: (0, 0)),
        ],
        out_specs=pl.BlockSpec((tm, c3), lambda i: (i, 0)),
        out_shape=jax.ShapeDtypeStruct((t, c3), F32),
        compiler_params=_cparams(("arbitrary",)),
        name="gdn_conv",
    )(proj, proj, proj, conv_w)


SCAN_RB = 512


def _tri_masks(reverse):
    ii = lax.broadcasted_iota(jnp.int32, (CHUNK, CHUNK), 0)
    jj = lax.broadcasted_iota(jnp.int32, (CHUNK, CHUNK), 1)
    if reverse:
        return jj >= ii, jj > ii
    return jj <= ii, jj < ii


MLSTM_CHUNKS_PER_ITER = 4


def _mlstm_local(chains):
    for ch in chains:
        incl, _ = _tri_masks(ch["reverse"])
        ch["d"] = jnp.where(incl, ch["bc_c"] - ch["bc_r"] + ch["ic_r"], -jnp.inf)
        ch["qb"] = ch["q"].astype(BF16)
        ch["vb"] = ch["v"].astype(BF16)
        bc_c = ch["bc_c"]
        ch["b_last"] = bc_c[0:1, :] if ch["reverse"] else bc_c[CHUNK - 1:CHUNK, :]
        ch["w_log"] = ch["b_last"] - bc_c + ch["ic_c"]
    for ch in chains:
        ch["qk"] = _dot_nt(ch["qb"], ch["k"].astype(BF16))
    for ch in chains:
        ch["rmax"] = jnp.max(ch["d"], axis=1, keepdims=True)
    for ch in chains:
        ch["wl_max"] = jnp.max(ch["w_log"], axis=0, keepdims=True)


def _mlstm_state(chains):
    for ch in chains:
        ch["inter"] = ch["bc_c"] + ch["m"]
    for ch in chains:
        ch["qc"] = _dot(ch["qb"], ch["c"].astype(BF16))
    for ch in chains:
        ch["qn"] = jnp.sum(ch["q"] * ch["n"], axis=1, keepdims=True)
    for ch in chains:
        ch["m_new"] = jnp.maximum(ch["b_last"] + ch["m"], ch["wl_max"])
        ch["carry"] = jnp.exp(ch["b_last"] + ch["m"] - ch["m_new"])
        ch["kw"] = ch["k"] * jnp.exp(ch["w_log"] - ch["m_new"])
    for ch in chains:
        ch["kv"] = _dot_tn(ch["kw"].astype(BF16), ch["vb"])
    for ch in chains:
        ch["m_row"] = jnp.maximum(ch["rmax"], ch["inter"])
        ch["s_inter"] = jnp.exp(ch["inter"] - ch["m_row"])
        ch["w"] = jnp.exp(ch["d"] - ch["m_row"]) * ch["qk"]
    for ch in chains:
        ch["wv"] = _dot(ch["w"].astype(BF16), ch["vb"])
    for ch in chains:
        ch["wsum"] = jnp.sum(ch["w"], axis=1, keepdims=True)
    for ch in chains:
        ch["c_new"] = ch["carry"] * ch["c"] + ch["kv"]
        ch["n_new"] = ch["carry"] * ch["n"] + jnp.sum(ch["kw"], axis=0, keepdims=True)
    for ch in chains:
        num = ch["wv"] + ch["s_inter"] * ch["qc"]
        den = ch["wsum"] + ch["s_inter"] * ch["qn"]
        ch["out"] = num / jnp.maximum(jnp.abs(den), jnp.exp(-ch["m_row"]))


def _mlstm_kernel(qf_ref, kf_ref, vf_ref, gcf_ref, grf_ref,
                  qb_ref, kb_ref, vb_ref, gcb_ref, grb_ref,
                  of_ref, ob_ref, c_scr, n_scr, m_scr):
    @pl.when(pl.program_id(0) == 0)
    def _():
        c_scr[...] = jnp.zeros_like(c_scr)
        n_scr[...] = jnp.zeros_like(n_scr)
        m_scr[...] = jnp.zeros_like(m_scr)

    ncb = qf_ref.shape[0] // CHUNK
    kscale = M_QK_DIM ** -0.5

    def load_chains(c):
        chains = []
        for reverse in (False, True):
            cc = (ncb - 1 - c) if reverse else c
            rows = pl.ds(pl.multiple_of(cc * CHUNK, CHUNK), CHUNK)
            q_ref, k_ref, v_ref, gc_ref, gr_ref, o_ref = (
                (qb_ref, kb_ref, vb_ref, gcb_ref, grb_ref, ob_ref) if reverse
                else (qf_ref, kf_ref, vf_ref, gcf_ref, grf_ref, of_ref))
            gc = gc_ref[rows, :]
            gr = gr_ref[cc]
            for h in range(M_HEADS):
                s = (1 if reverse else 0) * M_HEADS + h
                ci, cf = COL_I + s, COL_F + s
                chains.append(dict(
                    reverse=reverse, s=s, o_ref=o_ref, rows=rows, h=h,
                    q=q_ref[rows, h * M_QK_DIM:(h + 1) * M_QK_DIM],
                    k=k_ref[rows, h * M_QK_DIM:(h + 1) * M_QK_DIM] * kscale,
                    v=v_ref[rows, h * M_V_DIM:(h + 1) * M_V_DIM],
                    ic_c=gc[:, ci:ci + 1], bc_c=gc[:, cf:cf + 1],
                    ic_r=gr[ci:ci + 1, :], bc_r=gr[cf:cf + 1, :]))
        return chains

    def body(p, carry):
        groups = [load_chains(p * MLSTM_CHUNKS_PER_ITER + i) for i in range(MLSTM_CHUNKS_PER_ITER)]
        _mlstm_local([ch for g in groups for ch in g])
        ns = 2 * M_HEADS
        state = [(c_scr[s], n_scr[s], m_scr[s][:, 0:1]) for s in range(ns)]
        for g in groups:
            for ch in g:
                ch["c"], ch["n"], ch["m"] = state[ch["s"]]
            _mlstm_state(g)
            for ch in g:
                h = ch["h"]
                ch["o_ref"][ch["rows"], h * M_V_DIM:(h + 1) * M_V_DIM] = ch["out"]
                state[ch["s"]] = (ch["c_new"], ch["n_new"], ch["m_new"])
        for s in range(ns):
            c_scr[s] = state[s][0]
            n_scr[s] = state[s][1]
            m_scr[s] = jnp.broadcast_to(state[s][2], (1, 128))
        return carry

    lax.fori_loop(0, ncb // MLSTM_CHUNKS_PER_ITER, body, 0)


def _mlstm(proj, g_col, g_row):
    t = proj.shape[0]
    rb = min(SCAN_RB, t)
    nb = t // rb
    ncb = rb // CHUNK
    fwd = lambda s: s
    bwd = lambda s: nb - 1 - s

    def specs(rmap):
        return [
            pl.BlockSpec((rb, M_QK), lambda s: (rmap(s), OFF_MQ // M_QK)),
            pl.BlockSpec((rb, M_QK), lambda s: (rmap(s), OFF_MK // M_QK)),
            pl.BlockSpec((rb, M_V), lambda s: (rmap(s), OFF_MV // M_V)),
            pl.BlockSpec((rb, GATE_PAD), lambda s: (rmap(s), 0)),
            pl.BlockSpec((ncb, GATE_PAD, CHUNK), lambda s: (rmap(s), 0, 0)),
        ]

    ns = 2 * M_HEADS
    return pl.pallas_call(
        _mlstm_kernel,
        grid=(nb,),
        in_specs=specs(fwd) + specs(bwd),
        out_specs=[pl.BlockSpec((rb, M_V), lambda s: (s, 0)),
                   pl.BlockSpec((rb, M_V), lambda s: (nb - 1 - s, 0))],
        out_shape=[jax.ShapeDtypeStruct((t, M_V), F32)] * 2,
        scratch_shapes=[pltpu.VMEM((ns, M_QK_DIM, M_V_DIM), F32),
                        pltpu.VMEM((ns, 1, M_QK_DIM), F32),
                        pltpu.VMEM((ns, 1, 128), F32)],
        compiler_params=_cparams(("arbitrary",)),
        name="mlstm_scan",
    )(proj, proj, proj, g_col, g_row, proj, proj, proj, g_col, g_row)


GDN_CHUNKS_PER_ITER = 2


def _gdn_local(chains):
    ii = lax.broadcasted_iota(jnp.int32, (CHUNK, CHUNK), 0)
    jj = lax.broadcasted_iota(jnp.int32, (CHUNK, CHUNK), 1)
    eye = jnp.where(ii == jj, 1.0, 0.0)
    for ch in chains:
        incl, strict = _tri_masks(ch["reverse"])
        gc_c = ch["gc_c"]
        ch["strict"] = strict
        ch["decay"] = jnp.exp(jnp.where(incl, gc_c - ch["gc_r"], -jnp.inf))
        ch["kb16"] = ch["k"].astype(BF16)
        ch["kbeta"] = ch["k"] * ch["beta_c"]
        g_last = gc_c[0:1, :] if ch["reverse"] else gc_c[CHUNK - 1:CHUNK, :]
        ch["s_decay"] = jnp.exp(g_last)
        ch["kdec"] = (ch["k"] * jnp.exp(g_last - gc_c)).astype(BF16)
        ch["qg"] = (ch["q"] * jnp.exp(gc_c)).astype(BF16)
    for ch in chains:
        ch["kk"] = _dot_nt(ch["kbeta"].astype(BF16), ch["kb16"])
    for ch in chains:
        ch["attn"] = (_dot_nt(ch["q"].astype(BF16), ch["kb16"]) * ch["decay"]).astype(BF16)
    for ch in chains:
        ch["pw"] = -jnp.where(ch["strict"], ch["kk"] * ch["decay"], 0.0)
        ch["inv"] = eye + ch["pw"]
    for _ in range(5):
        for ch in chains:
            pwb = ch["pw"].astype(BF16)
            ch["pw"] = _dot(pwb, pwb)
        for ch in chains:
            ch["inv"] = ch["inv"] + _dot(ch["inv"].astype(BF16), ch["pw"].astype(BF16))
    for ch in chains:
        rhs = jnp.concatenate([ch["v"] * ch["beta_c"], ch["kbeta"] * jnp.exp(ch["gc_c"])], axis=1)
        sol = _dot(ch["inv"].astype(BF16), rhs.astype(BF16))
        ch["u"] = sol[:, :G_HEAD_DIM]
        ch["w"] = sol[:, G_HEAD_DIM:].astype(BF16)


def _gdn_state(chains):
    for ch in chains:
        ch["sb"] = ch["s"].astype(BF16)
    for ch in chains:
        ch["vnb"] = (ch["u"] - _dot(ch["w"], ch["sb"])).astype(BF16)
    for ch in chains:
        ch["qs"] = _dot(ch["qg"], ch["sb"])
    for ch in chains:
        ch["s_new"] = ch["s"] * ch["s_decay"] + _dot_tn(ch["kdec"], ch["vnb"])
    for ch in chains:
        ch["out"] = ch["qs"] + _dot(ch["attn"], ch["vnb"])


def _gdn_kernel(qf_ref, kf_ref, vf_ref, gcf_ref, grf_ref,
                qb_ref, kb_ref, vb_ref, gcb_ref, grb_ref,
                of_ref, ob_ref, s_scr):
    @pl.when(pl.program_id(0) == 0)
    def _():
        s_scr[...] = jnp.zeros_like(s_scr)

    ncb = qf_ref.shape[0] // CHUNK

    def load_chains(c):
        chains = []
        for reverse in (False, True):
            cc = (ncb - 1 - c) if reverse else c
            rows = pl.ds(pl.multiple_of(cc * CHUNK, CHUNK), CHUNK)
            q_ref, k_ref, v_ref, gc_ref, gr_ref, o_ref = (
                (qb_ref, kb_ref, vb_ref, gcb_ref, grb_ref, ob_ref) if reverse
                else (qf_ref, kf_ref, vf_ref, gcf_ref, grf_ref, of_ref))
            gc = gc_ref[rows, :]
            gr = gr_ref[cc]
            for h in range(G_HEADS):
                s = (1 if reverse else 0) * G_HEADS + h
                cb, ca = COL_B + s, COL_A + s
                cols = slice(h * G_HEAD_DIM, (h + 1) * G_HEAD_DIM)
                chains.append(dict(
                    reverse=reverse, slot=s, o_ref=o_ref, rows=rows, cols=cols,
                    q=q_ref[rows, cols], k=k_ref[rows, cols], v=v_ref[rows, cols],
                    beta_c=gc[:, cb:cb + 1], gc_c=gc[:, ca:ca + 1], gc_r=gr[ca:ca + 1, :]))
        return chains

    def body(p, carry):
        groups = [load_chains(p * GDN_CHUNKS_PER_ITER + i) for i in range(GDN_CHUNKS_PER_ITER)]
        _gdn_local([ch for g in groups for ch in g])
        state = [s_scr[s] for s in range(2 * G_HEADS)]
        for g in groups:
            for ch in g:
                ch["s"] = state[ch["slot"]]
            _gdn_state(g)
            for ch in g:
                ch["o_ref"][ch["rows"], ch["cols"]] = ch["out"]
                state[ch["slot"]] = ch["s_new"]
        for s in range(2 * G_HEADS):
            s_scr[s] = state[s]
        return carry

    lax.fori_loop(0, ncb // GDN_CHUNKS_PER_ITER, body, 0)


def _gdn(qkv, g_col, g_row):
    t = qkv.shape[0]
    rb = min(SCAN_RB, t)
    nb = t // rb
    ncb = rb // CHUNK
    fwd = lambda s: s
    bwd = lambda s: nb - 1 - s

    def specs(rmap):
        return [
            pl.BlockSpec((rb, G_W), lambda s: (rmap(s), 0)),
            pl.BlockSpec((rb, G_W), lambda s: (rmap(s), 1)),
            pl.BlockSpec((rb, G_W), lambda s: (rmap(s), 2)),
            pl.BlockSpec((rb, GATE_PAD), lambda s: (rmap(s), 0)),
            pl.BlockSpec((ncb, GATE_PAD, CHUNK), lambda s: (rmap(s), 0, 0)),
        ]

    return pl.pallas_call(
        _gdn_kernel,
        grid=(nb,),
        in_specs=specs(fwd) + specs(bwd),
        out_specs=[pl.BlockSpec((rb, G_W), lambda s: (s, 0)),
                   pl.BlockSpec((rb, G_W), lambda s: (nb - 1 - s, 0))],
        out_shape=[jax.ShapeDtypeStruct((t, G_W), F32)] * 2,
        scratch_shapes=[pltpu.VMEM((2 * G_HEADS, G_HEAD_DIM, G_HEAD_DIM), F32)],
        compiler_params=_cparams(("arbitrary",)),
        name="gdn_scan",
    )(qkv, qkv, qkv, g_col, g_row, qkv, qkv, qkv, g_col, g_row)


OP_TM = 256
OP_SUB = 128


def _outproj_kernel(mf_ref, mb_ref, gf_ref, gb_ref, mo_ref, gz_ref, x_ref, mod_ref,
                    mnw_ref, gnw_ref, wout_ref, ln_ref, rw_ref,
                    h1_ref, u2_ref, lg_ref):
    gate1, scale2, shift2 = mod_ref[0:1, :], mod_ref[1:2, :], mod_ref[2:3, :]
    subs = [slice(r, r + OP_SUB) for r in range(0, x_ref.shape[0], OP_SUB)]
    mixed, y, h1s, u2s = {}, {}, {}, {}
    for i, rows in enumerate(subs):
        hm = mf_ref[rows, :] + mb_ref[rows, :]
        hg = gf_ref[rows, :] + gb_ref[rows, :]
        parts = []
        for h in range(M_HEADS):
            seg = hm[:, h * M_V_DIM:(h + 1) * M_V_DIM]
            parts.append(seg * lax.rsqrt(jnp.mean(seg * seg, axis=-1, keepdims=True) + NORM_EPS))
        hm_n = jnp.concatenate(parts, axis=1) * mnw_ref[...] * _sigmoid(mo_ref[rows, :])
        parts = []
        for h in range(G_HEADS):
            seg = hg[:, h * G_HEAD_DIM:(h + 1) * G_HEAD_DIM]
            parts.append(seg * lax.rsqrt(jnp.mean(seg * seg, axis=-1, keepdims=True) + NORM_EPS))
        hg_n = jnp.concatenate(parts, axis=1) * gnw_ref[...] * _silu(gz_ref[rows, :])
        mixed[i] = jnp.concatenate([hm_n, hg_n], axis=1).astype(BF16)
    for i, rows in enumerate(subs):
        y[i] = _dot(mixed[i], wout_ref[...])
    for i, rows in enumerate(subs):
        h1 = (_layer_norm(DEEPNORM_ALPHA * x_ref[rows, :] + (1.0 + gate1) * y[i]) * ln_ref[0:1, :]
              + ln_ref[1:2, :])
        h1_ref[rows, :] = h1
        h1s[i] = h1
    for i, rows in enumerate(subs):
        u2 = _layer_norm(h1s[i]) * (1.0 + scale2) + shift2
        u2_ref[rows, :] = u2
        u2s[i] = u2
    for i, rows in enumerate(subs):
        lg_ref[:, rows] = lax.dot_general(rw_ref[...], u2s[i], (((1,), (1,)), ((), ())),
                                          precision=lax.Precision.HIGHEST, preferred_element_type=F32)


def _outproj(mf, mb, gf, gb, proj, x, mod3, mnw, gnw, w_out, ln1, rw_t):
    t = x.shape[0]
    tm = min(OP_TM, t)
    row = lambda i: (i, 0)
    const = lambda i: (0, 0)
    return pl.pallas_call(
        _outproj_kernel,
        grid=(t // tm,),
        in_specs=[
            pl.BlockSpec((tm, M_V), row), pl.BlockSpec((tm, M_V), row),
            pl.BlockSpec((tm, G_W), row), pl.BlockSpec((tm, G_W), row),
            pl.BlockSpec((tm, M_V), lambda i: (i, OFF_MO // M_V)),
            pl.BlockSpec((tm, G_W), lambda i: (i, OFF_GZ // G_W)),
            pl.BlockSpec((tm, D_MODEL), row),
            pl.BlockSpec((8, D_MODEL), const),
            pl.BlockSpec((1, M_V), const), pl.BlockSpec((1, G_W), const),
            pl.BlockSpec((D_MODEL, D_MODEL), const),
            pl.BlockSpec((8, D_MODEL), const),
            pl.BlockSpec((N_EXPERTS, D_MODEL), const),
        ],
        out_specs=[
            pl.BlockSpec((tm, D_MODEL), row),
            pl.BlockSpec((tm, D_MODEL), row),
            pl.BlockSpec((N_EXPERTS, tm), lambda i: (0, i)),
        ],
        out_shape=[
            jax.ShapeDtypeStruct((t, D_MODEL), F32),
            jax.ShapeDtypeStruct((t, D_MODEL), F32),
            jax.ShapeDtypeStruct((N_EXPERTS, t), F32),
        ],
        compiler_params=_cparams(("arbitrary",)),
        name="outproj_ln",
    )(mf, mb, gf, gb, proj, proj, x, mod3, mnw, gnw, w_out, ln1, rw_t)


RT_TN = 512
GROUP_SIZE = N_EXPERTS // N_GROUPS


def _route_kernel(lg_ref, bias_ref, idx_ref, w_ref, rank_ref, cnt_ref, carry_scr):
    tn = lg_ref.shape[1]

    @pl.when(pl.program_id(0) == 0)
    def _():
        carry_scr[...] = jnp.zeros_like(carry_scr)

    neg = -jnp.inf
    scores = _sigmoid(lg_ref[...])
    biased = scores + bias_ref[...]
    sub8 = lax.broadcasted_iota(jnp.int32, (GROUP_SIZE, tn), 0).astype(F32)
    gscore = []
    for g in range(N_GROUPS):
        bg = biased[g * GROUP_SIZE:(g + 1) * GROUP_SIZE, :]
        m1 = jnp.max(bg, axis=0, keepdims=True)
        first = jnp.min(jnp.where(bg == m1, sub8, float(GROUP_SIZE)), axis=0, keepdims=True)
        m2 = jnp.max(jnp.where(sub8 == first, neg, bg), axis=0, keepdims=True)
        gscore.append(m1 + m2)
    masked = []
    for g in range(N_GROUPS):
        beaten = jnp.zeros((1, tn), F32)
        for g2 in range(N_GROUPS):
            if g2 == g:
                continue
            wins = (gscore[g2] >= gscore[g]) if g2 < g else (gscore[g2] > gscore[g])
            beaten = beaten + jnp.where(wins, 1.0, 0.0)
        keep = beaten < float(TOPK_GROUPS)
        masked.append(jnp.where(keep, biased[g * GROUP_SIZE:(g + 1) * GROUP_SIZE, :], neg))
    x = jnp.concatenate(masked, axis=0)
    eidx = lax.broadcasted_iota(jnp.int32, (N_EXPERTS, tn), 0).astype(F32)
    member = jnp.zeros((N_EXPERTS, tn), F32)
    sel_idx, sel_s = [], []
    for _ in range(TOP_K):
        m = jnp.max(x, axis=0, keepdims=True)
        idx = jnp.min(jnp.where(x == m, eidx, float(N_EXPERTS)), axis=0, keepdims=True)
        sel = eidx == idx
        sel_idx.append(idx)
        sel_s.append(jnp.sum(jnp.where(sel, scores, 0.0), axis=0, keepdims=True))
        member = member + jnp.where(sel, 1.0, 0.0)
        x = jnp.where(sel, neg, x)
    total = sel_s[0]
    for s in sel_s[1:]:
        total = total + s
    ti = lax.broadcasted_iota(jnp.int32, (tn, tn), 0)
    tj = lax.broadcasted_iota(jnp.int32, (tn, tn), 1)
    before = jnp.where(ti < tj, 1.0, 0.0).astype(BF16)
    carry = carry_scr[:, 0:1]
    excl = _dot(member.astype(BF16), before) + carry
    ranks = [jnp.sum(jnp.where(eidx == i, excl, 0.0), axis=0, keepdims=True) for i in sel_idx]
    zero = jnp.zeros((1, tn), F32)
    idx_ref[...] = jnp.concatenate(sel_idx + [zero, zero], axis=0).astype(jnp.int32)
    w_ref[...] = jnp.concatenate([ROUTED_SCALE * s / total for s in sel_s] + [zero, zero], axis=0)
    rank_ref[...] = jnp.concatenate(ranks + [zero, zero], axis=0).astype(jnp.int32)
    new_carry = carry + jnp.sum(member, axis=1, keepdims=True)
    carry_scr[...] = jnp.broadcast_to(new_carry, carry_scr.shape)
    cnt_ref[...] = jnp.broadcast_to(new_carry, cnt_ref.shape).astype(jnp.int32)


def _route(logits_t, bias_col):
    t = logits_t.shape[1]
    tn = min(RT_TN, t)
    col = lambda i: (0, i)
    return pl.pallas_call(
        _route_kernel,
        grid=(t // tn,),
        in_specs=[pl.BlockSpec((N_EXPERTS, tn), col), pl.BlockSpec((N_EXPERTS, 1), lambda i: (0, 0))],
        out_specs=[pl.BlockSpec((8, tn), col), pl.BlockSpec((8, tn), col), pl.BlockSpec((8, tn), col),
                   pl.BlockSpec((N_EXPERTS, 128), lambda i: (0, 0))],
        out_shape=[jax.ShapeDtypeStruct((8, t), jnp.int32), jax.ShapeDtypeStruct((8, t), F32),
                   jax.ShapeDtypeStruct((8, t), jnp.int32), jax.ShapeDtypeStruct((N_EXPERTS, 128), jnp.int32)],
        scratch_shapes=[pltpu.VMEM((N_EXPERTS, 128), F32)],
        compiler_params=_cparams(("arbitrary",)),
        name="route_topk",
    )(logits_t, bias_col)


GMM_PAIRS = 2
GMM_HALF = EXPERT_BLOCK // 2


def _gmm_kernel(be_ref, nb_ref, first_ref, slot_ref, nxt_ref, half_ref, x_ref, w_hbm, *rest, mode):
    wbuf, sem = rest[-2], rest[-1]
    o_ref = rest[-3]
    tb = EXPERT_BLOCK
    nb = nb_ref[0]

    def weight_copy(e, s):
        return pltpu.make_async_copy(w_hbm.at[e], wbuf.at[s], sem.at[s])

    def enter_run(b):
        @pl.when(first_ref[b] == 1)
        def _():
            weight_copy(be_ref[b], slot_ref[b]).wait()

            @pl.when(nxt_ref[b] >= 0)
            def _():
                weight_copy(nxt_ref[b], 1 - slot_ref[b]).start()

    def zero(r0, nrows):
        o_ref[pl.ds(r0, nrows), :] = jnp.zeros((nrows, o_ref.shape[1]), o_ref.dtype)

    def compute(r0, nrows, b):
        rows = pl.ds(r0, nrows)
        acc = _dot(x_ref[rows, :].astype(BF16), wbuf[slot_ref[b]].astype(BF16))
        if mode == "silu":
            acc = _silu(acc)
        elif mode == "mul":
            acc = acc * rest[0][rows, :].astype(F32)
        o_ref[rows, :] = acc.astype(o_ref.dtype)

    def trimmed(r0, nrows, b_run, b_last):
        @pl.when(half_ref[b_last] == 0)
        def _():
            compute(r0, nrows, b_run)

        @pl.when(half_ref[b_last] == 1)
        def _():
            compute(r0, nrows - GMM_HALF, b_run)
            zero(r0 + nrows - GMM_HALF, GMM_HALF)

    def block(r0, b):
        @pl.when(b < nb)
        def _():
            trimmed(r0, tb, b, b)

        @pl.when(b >= nb)
        def _():
            zero(r0, tb)

    @pl.when(pl.program_id(0) == 0)
    def _():
        weight_copy(be_ref[0], 0).start()

    def pair_body(pair, carry):
        b0 = (pl.program_id(0) * GMM_PAIRS + pair) * 2
        b1 = b0 + 1
        r0 = pl.multiple_of(pair * 2 * tb, 2 * tb)
        enter_run(b0)
        same_run = (b1 < nb) & (first_ref[b1] == 0)

        @pl.when(same_run)
        def _():
            trimmed(r0, 2 * tb, b0, b1)

        @pl.when(jnp.logical_not(same_run))
        def _():
            block(r0, b0)
            enter_run(b1)
            block(r0 + tb, b1)

        return carry

    lax.fori_loop(0, GMM_PAIRS, pair_body, 0)


def _run_tables(block_e, n_used):
    nb = block_e.shape[0]
    idx = jnp.arange(nb, dtype=jnp.int32)
    used = idx < n_used[0]
    prev = jnp.concatenate([jnp.full((1,), -1, jnp.int32), block_e[:-1]])
    first = ((block_e != prev) & used).astype(jnp.int32)
    slot = (jnp.cumsum(first) - 1) & 1
    later_first = jnp.where(first == 1, idx, nb)
    nxt_pos = jnp.flip(lax.cummin(jnp.flip(jnp.concatenate([later_first[1:], jnp.full((1,), nb, jnp.int32)]))))
    nxt = jnp.where(nxt_pos < nb, jnp.take(block_e, jnp.minimum(nxt_pos, nb - 1)), -1)
    return first, slot.astype(jnp.int32), nxt.astype(jnp.int32)


def _gmm(x, w, tables, extra, mode, out_dtype):
    block_e, n_used, first, slot, nxt, half = tables
    rows, k = x.shape
    n = w.shape[2]
    bps = 2 * GMM_PAIRS
    tb = bps * EXPERT_BLOCK
    xmap = lambda p, be, nb, fi, sl, nx, hf: (jnp.minimum(p, (nb[0] - 1) // bps), 0)
    in_specs = [pl.BlockSpec((tb, k), xmap), pl.BlockSpec(memory_space=pl.ANY)]
    args = [x, w]
    if mode == "mul":
        in_specs.append(pl.BlockSpec((tb, n), xmap))
        args.append(extra)
    return pl.pallas_call(
        functools.partial(_gmm_kernel, mode=mode),
        grid_spec=pltpu.PrefetchScalarGridSpec(
            num_scalar_prefetch=6,
            grid=(rows // tb,),
            in_specs=in_specs,
            out_specs=pl.BlockSpec((tb, n), lambda p, be, nb, fi, sl, nx, hf: (p, 0)),
            scratch_shapes=[pltpu.VMEM((2, k, n), F32), pltpu.SemaphoreType.DMA((2,))],
        ),
        out_shape=jax.ShapeDtypeStruct((rows, n), out_dtype),
        compiler_params=_cparams(("arbitrary",)),
        name="gmm_" + mode,
    )(block_e, n_used, first, slot, nxt, half, *args)


def _swiglu_grouped(x, wg, wu, wd, tables):
    a = _gmm(x, wg, tables, None, "silu", BF16)
    h = _gmm(x, wu, tables, a, "mul", BF16)
    return _gmm(h, wd, tables, None, "plain", F32)


DSP_TT = 512
ROW_UNROLL = 8


def _dispatch_kernel(zf_ref, dest_ref, u_ref, xs_ref, zero_scr, sem, zsem):
    tt = u_ref.shape[0]

    @pl.when(pl.program_id(0) == 0)
    def _():
        zero_scr[...] = jnp.zeros_like(zero_scr)

        def block_fill(b):
            rows = pl.ds(pl.multiple_of(b * EXPERT_BLOCK, EXPERT_BLOCK), EXPERT_BLOCK)
            return pltpu.make_async_copy(zero_scr, xs_ref.at[rows], zsem)

        def fill_start(b, carry):
            @pl.when(zf_ref[b] != 0)
            def _():
                block_fill(b).start()
            return carry

        def fill_wait(b, carry):
            @pl.when(zf_ref[b] != 0)
            def _():
                block_fill(b).wait()
            return carry

        lax.fori_loop(0, zf_ref.shape[0], fill_start, 0)
        lax.fori_loop(0, zf_ref.shape[0], fill_wait, 0)

    def row_copy(t, k):
        return pltpu.make_async_copy(u_ref.at[pl.ds(t, 1)], xs_ref.at[pl.ds(dest_ref[k, t], 1)], sem)

    _row_dma_loops(tt, row_copy)


def _row_dma_loops(tt, row_copy, start=True, wait=True):
    def issue(g, carry):
        t0 = pl.multiple_of(g * ROW_UNROLL, ROW_UNROLL)
        for r in range(ROW_UNROLL):
            for k in range(TOP_K):
                row_copy(t0 + r, k).start(priority=k % 2)
        return carry

    def drain(g, carry):
        t0 = pl.multiple_of(g * ROW_UNROLL, ROW_UNROLL)
        for r in range(ROW_UNROLL):
            for k in range(TOP_K):
                row_copy(t0 + r, k).wait()
        return carry

    if start:
        lax.fori_loop(0, tt // ROW_UNROLL, issue, 0)
    if wait:
        lax.fori_loop(0, tt // ROW_UNROLL, drain, 0)


def _dispatch(u2, dest, zero_flag, n_slots):
    t = u2.shape[0]
    tt = min(DSP_TT, t)
    return pl.pallas_call(
        _dispatch_kernel,
        grid_spec=pltpu.PrefetchScalarGridSpec(
            num_scalar_prefetch=1,
            grid=(t // tt,),
            in_specs=[pl.BlockSpec((8, tt), lambda i, zf: (0, i), memory_space=pltpu.SMEM),
                      pl.BlockSpec((tt, D_MODEL), lambda i, zf: (i, 0))],
            out_specs=pl.BlockSpec(memory_space=pl.ANY),
            scratch_shapes=[pltpu.VMEM((EXPERT_BLOCK, D_MODEL), F32),
                            pltpu.SemaphoreType.DMA(()), pltpu.SemaphoreType.DMA(())],
        ),
        out_shape=jax.ShapeDtypeStruct((n_slots, D_MODEL), F32),
        compiler_params=_cparams(("arbitrary",)),
        name="moe_dispatch",
    )(zero_flag, dest, u2)


CMB_TT = 128


def _combine_kernel(dest_ref, dnext_ref, ys_ref, w_ref, h1_ref, sh_ref, mod_ref, ln_ref, o_ref, buf, sem):
    tt = h1_ref.shape[0]
    i = pl.program_id(0)
    slot = i % 2

    def row_copy(idx_ref, s):
        def make(t, k):
            return pltpu.make_async_copy(ys_ref.at[pl.ds(idx_ref[k, t], 1)],
                                         buf.at[s, k, pl.ds(t, 1)], sem.at[s])
        return make

    @pl.when(i == 0)
    def _():
        _row_dma_loops(tt, row_copy(dest_ref, slot), wait=False)

    @pl.when(i + 1 < pl.num_programs(0))
    def _():
        _row_dma_loops(tt, row_copy(dnext_ref, 1 - slot), wait=False)

    _row_dma_loops(tt, row_copy(dest_ref, slot), start=False)
    y = sh_ref[...]
    for k in range(TOP_K):
        y = y + buf[slot, k] * w_ref[:, k:k + 1]
    gate2 = mod_ref[3:4, :]
    o_ref[...] = (_layer_norm(DEEPNORM_ALPHA * h1_ref[...] + (1.0 + gate2) * y) * ln_ref[0:1, :]
                  + ln_ref[1:2, :])


def _combine(ys, dest, w_t, h1, shared, mod3, ln2):
    t = h1.shape[0]
    tt = min(CMB_TT, t)
    row = lambda i: (i, 0)
    const = lambda i: (0, 0)
    last = t // tt - 1
    return pl.pallas_call(
        _combine_kernel,
        grid=(t // tt,),
        in_specs=[pl.BlockSpec((8, tt), lambda i: (0, i), memory_space=pltpu.SMEM),
                  pl.BlockSpec((8, tt), lambda i: (0, jnp.minimum(i + 1, last)), memory_space=pltpu.SMEM),
                  pl.BlockSpec(memory_space=pl.ANY),
                  pl.BlockSpec((tt, 8), row),
                  pl.BlockSpec((tt, D_MODEL), row),
                  pl.BlockSpec((tt, D_MODEL), row),
                  pl.BlockSpec((8, D_MODEL), const),
                  pl.BlockSpec((8, D_MODEL), const)],
        out_specs=pl.BlockSpec((tt, D_MODEL), row),
        out_shape=jax.ShapeDtypeStruct((t, D_MODEL), F32),
        scratch_shapes=[pltpu.VMEM((2, TOP_K, tt, D_MODEL), F32), pltpu.SemaphoreType.DMA((2,))],
        compiler_params=_cparams(("arbitrary",)),
        name="moe_combine",
    )(dest, dest, ys, w_t, h1, shared, mod3, ln2)


def _pad_rows(rows, n=8):
    a = jnp.concatenate(rows, axis=0)
    return jnp.pad(a, ((0, n - a.shape[0]), (0, 0)))


def _mixer(x2, scale1, shift1, w_in, m_igate_bias, m_fgate_bias, g_conv_w, g_A_log, g_dt_bias):
    t = x2.shape[0]
    c0 = 2 * M_QK + 2 * M_V
    c1 = c0 + 4 * M_HEADS
    c2 = c1 + 4 * G_W
    w_main = jnp.concatenate([w_in[:, :c0], w_in[:, c1:c2]], axis=1).astype(BF16)
    w_gate = jnp.pad(jnp.concatenate([w_in[:, c0:c1], w_in[:, c2:]], axis=1),
                     ((0, 0), (0, GATE_PAD - N_GATE))).astype(BF16)
    proj, graw = _inproj(x2, scale1, shift1, w_main, w_gate)
    bias_row = jnp.pad(jnp.concatenate([m_igate_bias.reshape(-1), m_fgate_bias.reshape(-1),
                                        jnp.zeros((2 * G_HEADS,), F32), g_dt_bias.reshape(-1)]),
                       (0, GATE_PAD - N_GATE)).reshape(1, GATE_PAD)
    alog_row = jnp.pad(g_A_log.reshape(-1), (COL_A, GATE_PAD - N_GATE)).reshape(1, GATE_PAD)
    g_col = _gateprep(graw, bias_row, alog_row)
    g_row = jnp.swapaxes(g_col.reshape(t // CHUNK, CHUNK, GATE_PAD), 1, 2)
    mf, mb = _mlstm(proj, g_col, g_row)
    qkv = _gdn_conv(proj, jnp.pad(g_conv_w, ((0, 8 - CONV_WIDTH), (0, 0))))
    gf, gb = _gdn(qkv, g_col, g_row)
    return proj, mf, mb, gf, gb


def kernel(x, c, w_ada, b_ada, w_in, m_igate_bias, m_fgate_bias, m_norm_w, g_conv_w, g_A_log, g_dt_bias,
           g_norm_w, w_out, ln1_w, ln1_b, router_w, router_bias, e_gate, e_up, e_down, s_gate, s_up,
           s_down, ln2_w, ln2_b):
    bsz, t, d = x.shape
    assert bsz == 1 and d == D_MODEL and w_ada.shape[0] == 1
    x2 = x[0]
    mod = _ada_mod(c, w_ada[0], b_ada[0])
    shift1, scale1, gate1, shift2, scale2, gate2 = [mod[:, i * d:(i + 1) * d] for i in range(6)]
    proj, mf, mb, gf, gb = _mixer(x2, scale1, shift1, w_in[0], m_igate_bias[0], m_fgate_bias[0],
                                  g_conv_w[0], g_A_log[0], g_dt_bias[0])
    mod3 = _pad_rows([gate1, scale2, shift2, gate2])
    ln1 = _pad_rows([ln1_w[0][None], ln1_b[0][None]])
    ln2 = _pad_rows([ln2_w[0][None], ln2_b[0][None]])
    h1, u2, logits_t = _outproj(
        mf, mb, gf, gb, proj, x2, mod3, m_norm_w[0][None], jnp.tile(g_norm_w[0], G_HEADS)[None],
        w_out[0].astype(BF16), ln1, router_w[0].T)
    top_idx, top_w, rank, counts = _route(logits_t, router_bias[0][:, None])
    counts = counts[:, 0]
    n_slots = (t * TOP_K + N_EXPERTS * (EXPERT_BLOCK - 1) + EXPERT_BLOCK - 1) // EXPERT_BLOCK * EXPERT_BLOCK
    n_blocks = n_slots // EXPERT_BLOCK
    padded = (counts + EXPERT_BLOCK - 1) // EXPERT_BLOCK * EXPERT_BLOCK
    padded_end = jnp.cumsum(padded)
    group_start = padded_end - padded
    expert_ids = jnp.arange(N_EXPERTS, dtype=jnp.int32)[:, None, None]
    dest = jnp.sum(jnp.where(top_idx[None] == expert_ids, group_start[:, None, None], 0), axis=0) + rank
    n_used = (padded_end[-1:] // EXPERT_BLOCK).astype(jnp.int32)
    blk_all = jnp.arange(n_blocks, dtype=jnp.int32) * EXPERT_BLOCK
    blk_row = jnp.minimum(blk_all, padded_end[-1] - EXPERT_BLOCK)
    block_e = jnp.sum((padded_end[None, :] <= blk_row[:, None]).astype(jnp.int32), axis=1)
    bvalid = jnp.clip(jnp.take(group_start + counts, block_e) - blk_row, 0, EXPERT_BLOCK).astype(jnp.int32)
    zero_flag = ((bvalid < EXPERT_BLOCK) | (blk_all >= padded_end[-1])).astype(jnp.int32)
    xs = _dispatch(u2, dest, zero_flag, n_slots)
    half = ((bvalid <= GMM_HALF) & (blk_all < padded_end[-1])).astype(jnp.int32)
    ys = _swiglu_grouped(xs, e_gate[0], e_up[0], e_down[0],
                         (block_e, n_used) + _run_tables(block_e, n_used) + (half,))
    nsb = t // EXPERT_BLOCK
    shared_be, shared_nb = jnp.zeros((nsb,), jnp.int32), jnp.full((1,), nsb, jnp.int32)
    shared = _swiglu_grouped(u2, s_gate, s_up, s_down,
                             (shared_be, shared_nb) + _run_tables(shared_be, shared_nb) + (shared_be,))
    out = _combine(ys, dest, top_w.T, h1, shared, mod3, ln2)
    return out[None]
```
